```python
import jax, jax.numpy as jnp
from jax import lax
import numpy as np

D_MODEL = 1024
BATCH = 32
SEQ = 256
DEPTH = 2
DEC_BATCH = 2
DEC_SEQ = 1024
PAST_LEN = 512

GRID_W = 64
EPS = 1e-6
MLA_HEADS = 8
MLA_Q_RANK = 256
MLA_KV_RANK = 128
MLA_NOPE = 64
MLA_ROPE = 32
MLA_V = 64
ROPE_PAIRS = MLA_ROPE // 4
ROPE_THETA = 10000.0
Q_BLOCK = 128
FNO_GROUPS = 4
FNO_GROUP_CH = 64
FNO_WIDTH = FNO_GROUPS * FNO_GROUP_CH
GDN_HEADS = 4
GDN_DK = 64
GDN_DV = 64
GDN_CONV = 3
GDN_CHUNK = 64
N_EXPERTS = 32
TOP_K = 4
EXPERT_FF = D_MODEL
SWIGLU_ALPHA = 1.702
SWIGLU_LIMIT = 7.0
MOE_BLOCK = 128
N_BRANCHES = 3
IN_SPLITS = (MLA_Q_RANK, MLA_KV_RANK, MLA_ROPE, FNO_WIDTH,
             GDN_HEADS * GDN_DK, GDN_HEADS * GDN_DK, GDN_HEADS * GDN_DV, GDN_HEADS * GDN_DV,
             GDN_HEADS, GDN_HEADS, GDN_HEADS, GDN_HEADS, N_BRANCHES * D_MODEL)
IN_COLS = sum(IN_SPLITS)
GDN_QKV = 2 * GDN_HEADS * GDN_DK + GDN_HEADS * GDN_DV

kernel_name = 'hybrid_mla_fnet_gdn_moe_diffusion_step'


def rmsnorm(x, w):
    xf = x.astype(jnp.float32)
    y = xf * lax.rsqrt(jnp.mean(xf * xf, axis=-1, keepdims=True) + EPS)
    return (y * w.astype(jnp.float32)).astype(x.dtype)


def l2norm(x):
    return x * lax.rsqrt(jnp.sum(x * x, axis=-1, keepdims=True) + EPS)


def split_cols(a):
    idx = np.cumsum(IN_SPLITS)[:-1].tolist()
    return jnp.split(a, idx, axis=-1)


def axial_rope_tables(n_tokens):
    rows = n_tokens // GRID_W
    row = jnp.repeat(jnp.arange(rows, dtype=jnp.float32), GRID_W)
    col = jnp.tile(jnp.arange(GRID_W, dtype=jnp.float32), rows)
    inv = ROPE_THETA ** (-jnp.arange(ROPE_PAIRS, dtype=jnp.float32) / ROPE_PAIRS)
    ang = jnp.stack([row[:, None] * inv, col[:, None] * inv], axis=1)
    return jnp.cos(ang), jnp.sin(ang)


def apply_axial_rope(x, cos, sin):
    shp = x.shape
    xr = x.astype(jnp.float32).reshape(shp[:-1] + (2, 2, ROPE_PAIRS))
    extra = (1,) * (x.ndim - 3)
    cs = cos.reshape((cos.shape[0],) + extra + (2, ROPE_PAIRS))
    sn = sin.reshape((sin.shape[0],) + extra + (2, ROPE_PAIRS))
    x1, x2 = xr[..., 0, :], xr[..., 1, :]
    out = jnp.stack([x1 * cs - x2 * sn, x2 * cs + x1 * sn], axis=-2)
    return out.reshape(shp).astype(x.dtype)


def mla_attend(q_nope, q_pe, k_nope, k_pe, v):
    B, Tq = q_nope.shape[:2]
    nb = Tq // Q_BLOCK
    scale = (MLA_NOPE + MLA_ROPE) ** -0.5

    def block(qs):
        qn, qp = qs
        s = (jnp.einsum('bqhd,bkhd->bhqk', qn, k_nope)
             + jnp.einsum('bqhr,bkr->bhqk', qp, k_pe)).astype(jnp.float32) * scale
        prob = jax.nn.softmax(s, axis=-1).astype(v.dtype)
        return jnp.einsum('bhqk,bkhd->bqhd', prob, v)

    qn = q_nope.reshape(B, nb, Q_BLOCK, MLA_HEADS, MLA_NOPE).swapaxes(0, 1)
    qp = q_pe.reshape(B, nb, Q_BLOCK, MLA_HEADS, MLA_ROPE).swapaxes(0, 1)
    o = lax.map(block, (qn, qp))
    return o.swapaxes(0, 1).reshape(B, Tq, MLA_HEADS, MLA_V)


def fourier_mix(u, w_o):
    B, T, _ = u.shape
    ug = u.astype(jnp.float32).reshape(B, T, FNO_GROUPS, FNO_GROUP_CH)
    f = jnp.fft.fft2(ug, axes=(1, 3), norm='ortho').real
    return f.reshape(B, T, FNO_WIDTH).astype(u.dtype) @ w_o


def centred_dwconv(x, w):
    pad = GDN_CONV // 2
    return lax.conv_general_dilated(x, w[:, None, :].astype(x.dtype), window_strides=(1,),
                                    padding=[(pad, pad)], dimension_numbers=('NWC', 'WIO', 'NWC'),
                                    feature_group_count=x.shape[-1])


def gated_delta_chunked(q, k, v, g, beta, s0):
    B, T, H, DK = q.shape
    DV = v.shape[-1]
    C = GDN_CHUNK
    N = T // C

    def chunks(a):
        a = a.reshape((B, N, C, H) + a.shape[3:])
        return jnp.moveaxis(a, (1, 3), (0, 2))

    q, k, v, g, beta = chunks(q), chunks(k), chunks(v), chunks(g), chunks(beta)
    g = jnp.cumsum(g, axis=-1)
    kb = k * beta[..., None]
    vb = v * beta[..., None]
    idx = jnp.arange(C)
    causal = idx[:, None] >= idx[None, :]
    strict = idx[:, None] > idx[None, :]
    diff = g[..., :, None] - g[..., None, :]
    decay = jnp.where(causal, jnp.exp(jnp.where(causal, diff, 0.0)), 0.0)
    a_mat = jnp.where(strict, jnp.einsum('nbhcd,nbhed->nbhce', kb, k) * decay, 0.0)
    lhs = a_mat + jnp.eye(C, dtype=a_mat.dtype)
    u = lax.linalg.triangular_solve(lhs, vb, left_side=True, lower=True, unit_diagonal=True)
    w = lax.linalg.triangular_solve(lhs, kb * jnp.exp(g)[..., None], left_side=True, lower=True,
                                    unit_diagonal=True)
    qk = jnp.einsum('nbhcd,nbhed->nbhce', q, k) * decay

    def step(S, xs):
        qi, ki, ui, wi, gi, qki = xs
        v_new = ui - jnp.einsum('bhcd,bhde->bhce', wi, S)
        o = (jnp.einsum('bhcd,bhde->bhce', qi * jnp.exp(gi)[..., None], S)
             + jnp.einsum('bhce,bhef->bhcf', qki, v_new))
        g_last = gi[..., -1]
        S = (S * jnp.exp(g_last)[..., None, None]
             + jnp.einsum('bhcd,bhce->bhde', ki * jnp.exp(g_last[..., None] - gi)[..., None], v_new))
        return S, o

    S, o = lax.scan(step, s0, (q, k, u, w, g, qk))
    o = jnp.moveaxis(o, (0, 2), (1, 3)).reshape(B, T, H, DV)
    return o, S


def gdn_branch(q, k, v, z, b_f, b_b, a_f, a_b, s0, p):
    B, T, _ = q.shape
    qkv = jax.nn.silu(centred_dwconv(jnp.concatenate([q, k, v], axis=-1), p['gdn_conv_w']))
    qkv = qkv.astype(jnp.float32)
    nq = GDN_HEADS * GDN_DK
    qh = l2norm(qkv[..., :nq].reshape(B, T, GDN_HEADS, GDN_DK)) * (GDN_DK ** -0.5)
    kh = l2norm(qkv[..., nq:2 * nq].reshape(B, T, GDN_HEADS, GDN_DK))
    vh = qkv[..., 2 * nq:].reshape(B, T, GDN_HEADS, GDN_DV)
    A = jnp.exp(p['gdn_A_log'].astype(jnp.float32))
    dt_b = p['gdn_dt_bias'].astype(jnp.float32)
    g_f = -A[0] * jax.nn.softplus(a_f.astype(jnp.float32) + dt_b[0])
    g_b = -A[1] * jax.nn.softplus(a_b.astype(jnp.float32) + dt_b[1])
    beta_f = jax.nn.sigmoid(b_f.astype(jnp.float32))
    beta_b = jax.nn.sigmoid(b_b.astype(jnp.float32))
    s0 = s0.astype(jnp.float32)
    o_f, s_f = gated_delta_chunked(qh, kh, vh, g_f, beta_f, s0[:, 0])
    flip = lambda a: jnp.flip(a, axis=1)
    o_b, s_b = gated_delta_chunked(flip(qh), flip(kh), flip(vh), flip(g_b), flip(beta_b), s0[:, 1])
    o = o_f + flip(o_b)
    o = rmsnorm(o, p['gdn_onorm_w']) * jax.nn.silu(z.astype(jnp.float32).reshape(B, T, GDN_HEADS, GDN_DV))
    out = o.reshape(B, T, GDN_HEADS * GDN_DV).astype(z.dtype) @ p['gdn_w_o']
    return out, jnp.stack([s_f, s_b], axis=1)


def token_mixer(h, p, rope_cs, ctx):
    B, T, _ = h.shape
    (q_c, kv_c, k_r, u_f, q, k, v, z, b_f, b_b, a_f, a_b, gates) = split_cols(h @ p['w_in'])
    qa = (rmsnorm(q_c, p['mla_qn_w']) @ p['mla_w_uq']).reshape(B, T, MLA_HEADS, MLA_NOPE + MLA_ROPE)
    q_nope, q_pe = qa[..., :MLA_NOPE], qa[..., MLA_NOPE:]
    ckv = rmsnorm(kv_c, p['mla_kvn_w'])
    if ctx is None:
        keys_ckv, keys_pe = ckv, k_r
        s0 = jnp.zeros((B, 2, GDN_HEADS, GDN_DK, GDN_DV), jnp.float32)
    else:
        ckv_ctx, kpe_ctx, s0 = ctx
        cos, sin = rope_cs
        q_pe = apply_axial_rope(q_pe, cos, sin)
        keys_ckv = jnp.concatenate([ckv_ctx.astype(h.dtype), ckv], axis=1)
        keys_pe = jnp.concatenate([kpe_ctx.astype(h.dtype), apply_axial_rope(k_r, cos, sin)], axis=1)
    kv = (keys_ckv @ p['mla_w_ukv']).reshape(B, keys_ckv.shape[1], MLA_HEADS, MLA_NOPE + MLA_V)
    o_a = mla_attend(q_nope, q_pe, kv[..., :MLA_NOPE], keys_pe, kv[..., MLA_NOPE:])
    o_a = o_a.reshape(B, T, MLA_HEADS * MLA_V) @ p['mla_w_o']
    o_b = fourier_mix(u_f, p['fno_w_o'])
    o_c, s_fin = gdn_branch(q, k, v, z, b_f, b_b, a_f, a_b, s0, p)
    gate_a, gate_b, gate_c = jnp.split(jax.nn.sigmoid(gates), N_BRANCHES, axis=-1)
    out = (gate_a * o_a + gate_b * o_b + gate_c * o_c) @ p['w_out']
    return out, (ckv, k_r, s_fin)


def moe_ffn(h, p):
    B, T, Dm = h.shape
    x = h.reshape(-1, Dm)
    N = x.shape[0]
    logits = (x @ p['router_w'] + p['router_b']).astype(jnp.float32)
    top_v, top_i = lax.top_k(logits, TOP_K)
    gate = jax.nn.softmax(top_v, axis=-1)
    NK = N * TOP_K
    flat_e = top_i.reshape(NK)
    flat_t = jnp.arange(NK) // TOP_K
    flat_g = gate.reshape(NK)
    order = jnp.argsort(flat_e)
    sorted_e = flat_e[order]
    counts = jnp.zeros((N_EXPERTS,), jnp.int32).at[flat_e].add(1)
    padded = (counts + MOE_BLOCK - 1) // MOE_BLOCK * MOE_BLOCK
    pad_end = jnp.cumsum(padded)
    pad_start = pad_end - padded
    grp_start = jnp.cumsum(counts) - counts
    dest = pad_start[sorted_e] + jnp.arange(NK) - grp_start[sorted_e]
    n_blocks = -(-NK // MOE_BLOCK) + N_EXPERTS
    P = n_blocks * MOE_BLOCK
    slot_t = jnp.full((P,), N, jnp.int32).at[dest].set(flat_t[order])
    slot_g = jnp.zeros((P,), jnp.float32).at[dest].set(flat_g[order])
    block_e = jnp.minimum(jnp.searchsorted(pad_end, jnp.arange(n_blocks) * MOE_BLOCK, side='right'),
                          N_EXPERTS - 1)
    x_pad = jnp.concatenate([x, jnp.zeros((1, Dm), x.dtype)], axis=0)
    xb = x_pad[slot_t].reshape(n_blocks, MOE_BLOCK, Dm)
    w_gu, b_gu, w_dn, b_dn = p['moe_w_gu'], p['moe_b_gu'], p['moe_w_dn'], p['moe_b_dn']

    def expert_block(args):
        xblk, e = args
        gu = xblk @ w_gu[e] + b_gu[e]
        glu = jnp.minimum(gu[:, :EXPERT_FF], SWIGLU_LIMIT)
        lin = jnp.clip(gu[:, EXPERT_FF:], -SWIGLU_LIMIT, SWIGLU_LIMIT)
        act = glu * jax.nn.sigmoid(SWIGLU_ALPHA * glu) * (lin + 1)
        return act @ w_dn[e] + b_dn[e]

    yb = lax.map(expert_block, (xb, block_e)).reshape(P, Dm)
    y = jnp.zeros((N + 1, Dm), x.dtype).at[slot_t].add(yb * slot_g[:, None].astype(x.dtype))[:N]
    return y.reshape(B, T, Dm)


def trunk_layer(x, cond, p, rope_cs, ctx):
    mod = jax.nn.silu(cond) @ p['ada_w'] + p['ada_b']
    sh1, sc1, gt1, sh2, sc2, gt2 = [m[:, None, :] for m in jnp.split(mod, 6, axis=-1)]
    h = rmsnorm(x, p['norm1_w']) * (1 + sc1) + sh1
    mix, extras = token_mixer(h, p, rope_cs, ctx)
    x = x + gt1 * mix
    h = rmsnorm(x, p['norm2_w']) * (1 + sc2) + sh2
    x = x + gt2 * moe_ffn(h, p)
    return x, extras


def setup_inputs(seed: int = 0) -> dict:
    key = jax.random.key(seed)
    ks = list(jax.random.split(key, 32))

    def nrm(i, shape, scale):
        return jax.random.normal(ks[i], shape, jnp.float32) * scale

    D = D_MODEL
    dt = jnp.exp(jax.random.uniform(ks[20], (DEPTH, 2, GDN_HEADS), jnp.float32,
                                    minval=float(np.log(1e-3)), maxval=float(np.log(1e-1))))
    return {
        'x_prompt': nrm(0, (BATCH, SEQ, D), 1.0),
        'x_sample': nrm(1, (DEC_BATCH, DEC_SEQ, D), 1.0),
        'cache_ckv': nrm(2, (DEC_BATCH, DEPTH, PAST_LEN, MLA_KV_RANK), 1.0),
        'cache_kpe': nrm(3, (DEC_BATCH, DEPTH, PAST_LEN, MLA_ROPE), 1.0),
        'state_delta': nrm(4, (DEC_BATCH, DEPTH, 2, GDN_HEADS, GDN_DK, GDN_DV), GDN_DK ** -0.5),
        'c': nrm(5, (DEC_BATCH, D), 1.0),
        'c_ctx': nrm(6, (D,), 1.0),
        'ada_w': nrm(7, (DEPTH, D, 6 * D), 0.5 * D ** -0.5),
        'ada_b': nrm(8, (DEPTH, 6 * D), 0.02),
        'norm1_w': 1.0 + nrm(9, (DEPTH, D), 0.05),
        'norm2_w': 1.0 + nrm(10, (DEPTH, D), 0.05),
        'w_in': nrm(11, (DEPTH, D, IN_COLS), D ** -0.5),
        'mla_qn_w': 1.0 + nrm(12, (DEPTH, MLA_Q_RANK), 0.05),
        'mla_w_uq': nrm(13, (DEPTH, MLA_Q_RANK, MLA_HEADS * (MLA_NOPE + MLA_ROPE)), MLA_Q_RANK ** -0.5),
        'mla_kvn_w': 1.0 + nrm(14, (DEPTH, MLA_KV_RANK), 0.05),
        'mla_w_ukv': nrm(15, (DEPTH, MLA_KV_RANK, MLA_HEADS * (MLA_NOPE + MLA_V)), MLA_KV_RANK ** -0.5),
        'mla_w_o': nrm(16, (DEPTH, MLA_HEADS * MLA_V, D), (MLA_HEADS * MLA_V) ** -0.5),
        'fno_w_o': nrm(17, (DEPTH, FNO_WIDTH, D), FNO_WIDTH ** -0.5),
        'gdn_conv_w': nrm(18, (DEPTH, GDN_CONV, GDN_QKV), GDN_CONV ** -0.5),
        'gdn_A_log': jnp.log(jax.random.uniform(ks[19], (DEPTH, 2, GDN_HEADS), jnp.float32,
                                                minval=1.0, maxval=16.0)),
        'gdn_dt_bias': dt + jnp.log(-jnp.expm1(-dt)),
        'gdn_onorm_w': 1.0 + nrm(21, (DEPTH, GDN_DV), 0.05),
        'gdn_w_o': nrm(22, (DEPTH, GDN_HEADS * GDN_DV, D), (GDN_HEADS * GDN_DV) ** -0.5),
        'w_out': nrm(23, (DEPTH, D, D), D ** -0.5),
        'router_w': nrm(24, (DEPTH, D, N_EXPERTS), D ** -0.5),
        'router_b': nrm(25, (DEPTH, N_EXPERTS), 0.01),
        'moe_w_gu': nrm(26, (DEPTH, N_EXPERTS, D, 2 * EXPERT_FF), D ** -0.5),
        'moe_b_gu': nrm(27, (DEPTH, N_EXPERTS, 2 * EXPERT_FF), 0.01),
        'moe_w_dn': nrm(28, (DEPTH, N_EXPERTS, EXPERT_FF, D), EXPERT_FF ** -0.5),
        'moe_b_dn': nrm(29, (DEPTH, N_EXPERTS, D), 0.01),
        'final_norm_w': 1.0 + nrm(30, (D,), 0.05),
    }


def reference(x_prompt, x_sample, cache_ckv, cache_kpe, state_delta, c, c_ctx, ada_w, ada_b,
              norm1_w, norm2_w, w_in, mla_qn_w, mla_w_uq, mla_kvn_w, mla_w_ukv, mla_w_o, fno_w_o,
              gdn_conv_w, gdn_A_log, gdn_dt_bias, gdn_onorm_w, gdn_w_o, w_out, router_w, router_b,
              moe_w_gu, moe_b_gu, moe_w_dn, moe_b_dn, final_norm_w):
    rope_cs = axial_rope_tables(x_sample.shape[1])
    cond_ctx = c_ctx[None, :]
    xp, xs = x_prompt, x_sample
    ckvs, kpes, states = [], [], []
    for l in range(DEPTH):
        p = {
            'ada_w': ada_w[l], 'ada_b': ada_b[l], 'norm1_w': norm1_w[l], 'norm2_w': norm2_w[l],
            'w_in': w_in[l], 'mla_qn_w': mla_qn_w[l], 'mla_w_uq': mla_w_uq[l], 'mla_kvn_w': mla_kvn_w[l],
            'mla_w_ukv': mla_w_ukv[l], 'mla_w_o': mla_w_o[l], 'fno_w_o': fno_w_o[l],
            'gdn_conv_w': gdn_conv_w[l], 'gdn_A_log': gdn_A_log[l], 'gdn_dt_bias': gdn_dt_bias[l],
            'gdn_onorm_w': gdn_onorm_w[l], 'gdn_w_o': gdn_w_o[l], 'w_out': w_out[l],
            'router_w': router_w[l], 'router_b': router_b[l], 'moe_w_gu': moe_w_gu[l],
            'moe_b_gu': moe_b_gu[l], 'moe_w_dn': moe_w_dn[l], 'moe_b_dn': moe_b_dn[l],
        }
        xp, (ckv, kpe, st) = trunk_layer(xp, cond_ctx, p, None, None)
        ckvs.append(ckv)
        kpes.append(kpe)
        states.append(st)
        xs, _ = trunk_layer(xs, c, p, rope_cs, (cache_ckv[:, l], cache_kpe[:, l], state_delta[:, l]))
    y_prompt = rmsnorm(xp, final_norm_w)
    y_sample = rmsnorm(xs, final_norm_w)
    new_ckv = jnp.stack(ckvs, axis=1)
    new_kpe = jnp.stack(kpes, axis=1)
    new_state = jnp.stack(states, axis=1)
    return (y_prompt, y_sample, new_ckv, new_kpe, new_state)
```

```python
import functools

import numpy as np
import jax
import jax.numpy as jnp
from jax import lax
from jax.experimental import pallas as pl
from jax.experimental.pallas import tpu as pltpu

F32 = jnp.float32
BF16 = jnp.bfloat16
HI = lax.Precision.HIGHEST

D = 1024
BATCH, SEQ = 32, 256
DEC_BATCH, DEC_SEQ = 2, 1024
DEPTH = 2
PAST = 512
GRID_W = 64
EPS = 1e-6
N_PROMPT = BATCH * SEQ
N_SAMPLE = DEC_BATCH * DEC_SEQ
N_TOK = N_PROMPT + N_SAMPLE
ROWS = 256
N_RB = N_TOK // ROWS
N_RB_PROMPT = N_PROMPT // ROWS
RB_PER_SAMPLE = DEC_SEQ // ROWS

H_MLA, Q_RANK, KV_RANK, NOPE, ROPE, DV_MLA = 8, 256, 128, 64, 32, 64
ROPE_PAIRS = ROPE // 4
ROPE_THETA = 10000.0
QB = 256
FNO_G, FNO_C = 4, 64
FNO_W = FNO_G * FNO_C
H_GDN, DK, DVG, CONV, CHUNK = 4, 64, 64, 3, 64
HP = 128
N_EXP, TOP_K, FF = 32, 4, 1024
ALPHA, LIMIT = 1.702, 7.0
BM = 256
NK = N_TOK * TOP_K
NB = NK // BM + N_EXP
P_ROWS = NB * BM

OFF_GATES, OFF_QC, OFF_MISC, OFF_UF = 0, 3072, 3328, 3584
OFF_GQ, OFF_GK, OFF_GV, OFF_GZ = 3840, 4352, 4864, 5376
PC = 5888
PC_HALF = PC // 2
VMEM_LIMIT = 52 * 1024 * 1024


def _cparams(sem):
    return pltpu.CompilerParams(dimension_semantics=sem, vmem_limit_bytes=VMEM_LIMIT)


def _sigmoid(x):
    return 1.0 / (1.0 + jnp.exp(-x))


def _softplus(x):
    return jnp.maximum(x, 0.0) + jnp.log(1.0 + jnp.exp(-jnp.abs(x)))


def _dot(a, b):
    return jnp.dot(a.astype(BF16), b.astype(BF16), preferred_element_type=F32)


def _dot_hi(a, b):
    return jnp.dot(a, b, preferred_element_type=F32, precision=HI)


def _dot_nt(a, b):
    return lax.dot_general(a.astype(BF16), b.astype(BF16), (((1,), (1,)), ((), ())),
                           preferred_element_type=F32)


def _dot_tn(a, b):
    return lax.dot_general(a.astype(BF16), b.astype(BF16), (((0,), (0,)), ((), ())),
                           preferred_element_type=F32)


def _rms(x, n=None):
    n = x.shape[-1] if n is None else n
    return x * lax.rsqrt(jnp.sum(x * x, axis=-1, keepdims=True) * (1.0 / n) + EPS)


def _cond_index(i):
    return jnp.where(i < N_RB_PROMPT, 0, 1 + (i - N_RB_PROMPT) // RB_PER_SAMPLE)


def _ada_kernel(c_ref, w_ref, b_ref, o_ref):
    c = c_ref[...]
    o_ref[...] = _dot_hi(c * _sigmoid(c), w_ref[...]) + b_ref[...]


def _ada(cond8, ada_w, ada_b):
    nc = 4
    cb = 6 * D // nc
    return pl.pallas_call(
        _ada_kernel,
        grid=(DEPTH, nc),
        in_specs=[pl.BlockSpec((8, D), lambda l, j: (0, 0)),
                  pl.BlockSpec((None, D, cb), lambda l, j: (l, 0, j)),
                  pl.BlockSpec((None, 1, cb), lambda l, j: (l, 0, j))],
        out_specs=pl.BlockSpec((None, 8, cb), lambda l, j: (l, 0, j)),
        out_shape=jax.ShapeDtypeStruct((DEPTH, 8, 6 * D), F32),
        compiler_params=_cparams(("arbitrary", "arbitrary")),
        name="ada_mod",
    )(cond8, ada_w, ada_b.reshape(DEPTH, 1, 6 * D))


def _mod_spec(which, two_d=False):
    if two_d:
        return pl.BlockSpec((None, None, 1, D), lambda j, i: (_cond_index(i), which, 0, 0))
    return pl.BlockSpec((None, None, 1, D), lambda i: (_cond_index(i), which, 0, 0))


def _proj_in_kernel(x_ref, nw_ref, sh_ref, sc_ref, w_ref, o_ref):
    h = _rms(x_ref[...]) * nw_ref[...]
    h = h * (1.0 + sc_ref[...]) + sh_ref[...]
    o_ref[...] = jnp.dot(h.astype(BF16), w_ref[...], preferred_element_type=F32)


def _proj_in(x, norm_w, mod, w_packed):
    return pl.pallas_call(
        _proj_in_kernel,
        grid=(2, N_RB),
        in_specs=[pl.BlockSpec((ROWS, D), lambda j, i: (i, 0)),
                  pl.BlockSpec((1, D), lambda j, i: (0, 0)),
                  _mod_spec(0, True), _mod_spec(1, True),
                  pl.BlockSpec((D, PC_HALF), lambda j, i: (0, j))],
        out_specs=pl.BlockSpec((ROWS, PC_HALF), lambda j, i: (i, j)),
        out_shape=jax.ShapeDtypeStruct((N_TOK, PC), F32),
        compiler_params=_cparams(("arbitrary", "arbitrary")),
        name="proj_in",
    )(x, norm_w, mod, mod, w_packed)


def _make_mla_kernel(T, has_ctx):
    scale = (NOPE + ROPE) ** -0.5

    def kern(*refs):
        if has_ctx:
            (qc_ref, misc_ref, cckv_ref, ckpe_ref, qnw_ref, wuq_ref, kvnw_ref, wukv_ref, wo_ref,
             cq_ref, sq_ref, ck_ref, sk_ref, o_ref) = refs
        else:
            (qc_ref, misc_ref, qnw_ref, wuq_ref, kvnw_ref, wukv_ref, wo_ref,
             o_ref, ckv_out, kpe_out) = refs
        qn = _rms(qc_ref[...]) * qnw_ref[...]
        qa = jnp.dot(qn.astype(BF16), wuq_ref[...], preferred_element_type=F32)
        misc = misc_ref[...]
        ckv = _rms(misc[:, 0:KV_RANK]) * kvnw_ref[...]
        kr = misc[:, KV_RANK:KV_RANK + ROPE]
        q_pe = qa[:, 512:768]
        if has_ctx:
            q_pe = q_pe * cq_ref[...] + qa[:, 768:1024] * sq_ref[...]
            kr = kr * ck_ref[...] + misc[:, KV_RANK + ROPE:KV_RANK + 2 * ROPE] * sk_ref[...]
            keys_ckv = jnp.concatenate([cckv_ref[...], ckv], axis=0)
            keys_pe = jnp.concatenate([ckpe_ref[...], kr], axis=0)
        else:
            keys_ckv, keys_pe = ckv, kr
            ckv_out[...] = ckv
            kpe_out[...] = kr
        kv = jnp.dot(keys_ckv.astype(BF16), wukv_ref[...], preferred_element_type=F32)
        q_nope = qa[:, 0:512].astype(BF16)
        q_pe = q_pe.astype(BF16)
        kpe_b = keys_pe.astype(BF16)
        kv_b = kv.astype(BF16)
        for qb in range(T // QB):
            r = slice(qb * QB, (qb + 1) * QB)
            acc = jnp.zeros((QB, D), F32)
            for h in range(H_MLA):
                s = (_dot_nt(q_nope[r, NOPE * h:NOPE * (h + 1)], kv_b[:, NOPE * h:NOPE * (h + 1)])
                     + _dot_nt(q_pe[r, ROPE * h:ROPE * (h + 1)], kpe_b)) * scale
                p = jnp.exp(s - jnp.max(s, axis=-1, keepdims=True))
                den = jnp.sum(p, axis=-1, keepdims=True)
                o_h = _dot(p, kv_b[:, 512 + DV_MLA * h:512 + DV_MLA * (h + 1)]) / den
                acc = acc + jnp.dot(o_h.astype(BF16), wo_ref[DV_MLA * h:DV_MLA * (h + 1), :],
                                    preferred_element_type=F32)
            o_ref[r, :] = acc

    return kern


def _mla(P, l, T, nseq, rb0, ctx, w, rope):
    has_ctx = ctx is not None
    qc_blk, misc_blk = OFF_QC // 256, OFF_MISC // 256
    full = lambda shape: pl.BlockSpec(shape, lambda b: (0,) * len(shape))
    in_specs = [pl.BlockSpec((T, 256), lambda b: (rb0 + b, qc_blk)),
                pl.BlockSpec((T, 256), lambda b: (rb0 + b, misc_blk))]
    args = [P, P]
    if has_ctx:
        in_specs += [pl.BlockSpec((None, None, PAST, KV_RANK), lambda b: (b, l, 0, 0)),
                     pl.BlockSpec((None, None, PAST, ROPE), lambda b: (b, l, 0, 0))]
        args += [ctx[0], ctx[1]]
    in_specs += [full((1, Q_RANK)), full((Q_RANK, 1024)), full((1, KV_RANK)), full((KV_RANK, 1024)),
                 full((512, D))]
    args += [w['qn'], w['wuq'], w['kvn'], w['wukv'], w['wo']]
    out_shape = [jax.ShapeDtypeStruct((nseq * T, D), F32)]
    out_specs = [pl.BlockSpec((T, D), lambda b: (b, 0))]
    if has_ctx:
        in_specs += [full((T, 256)), full((T, 256)), full((T, ROPE)), full((T, ROPE))]
        args += list(rope)
    else:
        out_shape += [jax.ShapeDtypeStruct((nseq, T, KV_RANK), F32),
                      jax.ShapeDtypeStruct((nseq, T, ROPE), F32)]
        out_specs += [pl.BlockSpec((None, T, KV_RANK), lambda b: (b, 0, 0)),
                      pl.BlockSpec((None, T, ROPE), lambda b: (b, 0, 0))]
    return pl.pallas_call(
        _make_mla_kernel(T, has_ctx),
        grid=(nseq,),
        in_specs=in_specs, out_specs=out_specs, out_shape=out_shape,
        compiler_params=_cparams(("arbitrary",)),
        name="mla_ctx" if has_ctx else "mla_prompt",
    )(*args)


def _make_fnet_kernel(T):
    scale = 1.0 / float(np.sqrt(T * FNO_C))

    def kern(u_ref, cc_ref, sc_ref, ct_ref, st_ref, wo_ref, o_ref):
        u = u_ref[...]
        a = _dot_hi(u, cc_ref[...])
        b = _dot_hi(u, sc_ref[...])
        f = (_dot_hi(ct_ref[...], a) - _dot_hi(st_ref[...], b)) * scale
        o_ref[...] = jnp.dot(f.astype(BF16), wo_ref[...], preferred_element_type=F32)

    return kern


def _fnet(P, T, nseq, rb0, tabs, wo):
    full = lambda shape: pl.BlockSpec(shape, lambda b: (0,) * len(shape))
    return pl.pallas_call(
        _make_fnet_kernel(T),
        grid=(nseq,),
        in_specs=[pl.BlockSpec((T, FNO_W), lambda b: (rb0 + b, OFF_UF // 256)),
                  full((FNO_W, FNO_W)), full((FNO_W, FNO_W)), full((T, T)), full((T, T)),
                  full((FNO_W, D))],
        out_specs=pl.BlockSpec((T, D), lambda b: (b, 0)),
        out_shape=jax.ShapeDtypeStruct((nseq * T, D), F32),
        compiler_params=_cparams(("arbitrary",)),
        name=f"fnet_{T}",
    )(P, tabs[0], tabs[1], tabs[2], tabs[3], wo)


def _make_gdn_kernel(T, has_ctx):
    n = T // CHUNK
    C = CHUNK

    def kern(*refs):
        if has_ctx:
            (q_ref, k_ref, v_ref, z_ref, cw_ref, bcol_ref, brow_ref, par_ref, onw_ref, wo_ref, s0_ref,
             oc_ref, qh_s, kh_s, vh_s, g_s, b_s, o_s) = refs
        else:
            (q_ref, k_ref, v_ref, z_ref, cw_ref, bcol_ref, brow_ref, par_ref, onw_ref, wo_ref,
             oc_ref, sfin_ref, qh_s, kh_s, vh_s, g_s, b_s, o_s) = refs
        h = pl.program_id(1)
        rows = lax.broadcasted_iota(jnp.int32, (T, 1), 0)
        cw = cw_ref[...]

        def conv(x_ref, seg):
            x = x_ref[...]
            xp = jnp.where(rows == 0, 0.0, pltpu.roll(x, 1, 0))
            xn = jnp.where(rows == T - 1, 0.0, pltpu.roll(x, T - 1, 0))
            y = (xp * cw[3 * seg:3 * seg + 1] + x * cw[3 * seg + 1:3 * seg + 2]
                 + xn * cw[3 * seg + 2:3 * seg + 3])
            return y * _sigmoid(y)

        yq = conv(q_ref, 0)
        qh_s[...] = yq * lax.rsqrt(jnp.sum(yq * yq, axis=-1, keepdims=True) + EPS) * (DK ** -0.5)
        yk = conv(k_ref, 1)
        kh_s[...] = yk * lax.rsqrt(jnp.sum(yk * yk, axis=-1, keepdims=True) + EPS)
        vh_s[...] = conv(v_ref, 2)

        par = par_ref[...]
        bc = bcol_ref[...]
        ones = jnp.ones((1, HP), F32)
        for d in range(2):
            g = -jnp.exp(par[:, d:d + 1]) * _softplus(bc[:, 2 + d:3 + d] + par[:, 2 + d:3 + d])
            g_s[d] = g * ones
            b_s[d] = _sigmoid(bc[:, d:d + 1]) * ones

        ii = lax.broadcasted_iota(jnp.int32, (C, C), 0)
        jj = lax.broadcasted_iota(jnp.int32, (C, C), 1)
        lower = (ii >= jj).astype(F32)
        upper = (ii <= jj).astype(F32)
        eye = (ii == jj).astype(F32)

        def chunk_step(d, c, S):
            r0 = pl.multiple_of(c * C, C)
            q = qh_s[pl.ds(r0, C), :]
            k = kh_s[pl.ds(r0, C), :]
            v = vh_s[pl.ds(r0, C), :]
            gcol = g_s[d, pl.ds(r0, C), :]
            beta = b_s[d, pl.ds(r0, C), :]
            br = brow_ref[c]
            grow = -jnp.exp(par[:, d:d + 1]) * _softplus(br[2 + d:3 + d, :] + par[:, 2 + d:3 + d])
            grow8 = jnp.broadcast_to(grow, (8, C))
            if d == 0:
                gc = _dot_hi(lower, gcol)
                gr = _dot_hi(grow8, upper)[0:1, :]
                incl, strict = ii >= jj, ii > jj
            else:
                gc = _dot_hi(upper, gcol)
                gr = _dot_hi(grow8, lower)[0:1, :]
                incl, strict = ii <= jj, ii < jj
            diff = gc[:, 0:C] - gr
            decay = jnp.where(incl, jnp.exp(jnp.where(incl, diff, 0.0)), 0.0)
            kb = k * beta
            vb = v * beta
            a = jnp.where(strict, _dot_nt(kb, k) * decay, 0.0)
            x = eye - a
            p = _dot_hi(a, a)
            for it in range(5):
                x = x + _dot_hi(x, p)
                if it < 4:
                    p = _dot_hi(p, p)
            eg = jnp.exp(gc)
            u = _dot_hi(x, vb)
            w = _dot_hi(x, kb * eg)
            qk = _dot_nt(q, k) * decay
            v_new = u - _dot(w, S)
            o = _dot(q * eg, S) + _dot(qk, v_new)
            glast = gc[C - 1:C, :] if d == 0 else gc[0:1, :]
            S = S * jnp.exp(glast) + _dot_tn(k * jnp.exp(glast - gc), v_new)
            if d == 0:
                o_s[pl.ds(r0, C), :] = o
            else:
                o_s[pl.ds(r0, C), :] = o_s[pl.ds(r0, C), :] + o
            return S

        finals = []
        for d in range(2):
            S0 = s0_ref[d] if has_ctx else jnp.zeros((HP, HP), F32)
            if d == 0:
                S = lax.fori_loop(0, n, lambda c, S: chunk_step(0, c, S), S0)
            else:
                S = lax.fori_loop(0, n, lambda c, S: chunk_step(1, n - 1 - c, S), S0)
            finals.append(S)
        if not has_ctx:
            sfin_ref[0] = finals[0][0:DK, 0:DVG]
            sfin_ref[1] = finals[1][0:DK, 0:DVG]

        o = o_s[...]
        z = z_ref[...]
        y = _rms(o, DVG) * onw_ref[...] * (z * _sigmoid(z))
        val = jnp.dot(y.astype(BF16), wo_ref[...], preferred_element_type=F32)

        @pl.when(h == 0)
        def _():
            oc_ref[...] = val

        @pl.when(h != 0)
        def _():
            oc_ref[...] = oc_ref[...] + val

    return kern


def _gdn(P, l, T, nseq, rb0, s0, w, bcol, brow):
    has_ctx = s0 is not None
    n = T // CHUNK
    hb = lambda off: pl.BlockSpec((T, HP), lambda b, h: (rb0 + b, off // HP + h))
    in_specs = [hb(OFF_GQ), hb(OFF_GK), hb(OFF_GV), hb(OFF_GZ),
                pl.BlockSpec((None, 9, HP), lambda b, h: (h, 0, 0)),
                pl.BlockSpec((None, T, 4), lambda b, h: (h, rb0 + b, 0)),
                pl.BlockSpec((None, n, 4, CHUNK), lambda b, h: (h, rb0 + b, 0, 0)),
                pl.BlockSpec((None, 1, 4), lambda b, h: (h, 0, 0)),
                pl.BlockSpec((1, HP), lambda b, h: (0, 0)),
                pl.BlockSpec((None, HP, D), lambda b, h: (h, 0, 0))]
    args = [P, P, P, P, w['cw'], bcol, brow, w['par'], w['onw'], w['wo']]
    out_shape = [jax.ShapeDtypeStruct((nseq * T, D), F32)]
    out_specs = [pl.BlockSpec((T, D), lambda b, h: (b, 0))]
    if has_ctx:
        in_specs.append(pl.BlockSpec((None, None, 2, None, HP, HP), lambda b, h: (b, l, 0, h, 0, 0)))
        args.append(s0)
    else:
        out_shape.append(jax.ShapeDtypeStruct((nseq, 2, H_GDN, DK, DVG), F32))
        out_specs.append(pl.BlockSpec((None, 2, None, DK, DVG), lambda b, h: (b, 0, h, 0, 0)))
    return pl.pallas_call(
        _make_gdn_kernel(T, has_ctx),
        grid=(nseq, H_GDN),
        in_specs=in_specs, out_specs=out_specs, out_shape=out_shape,
        scratch_shapes=[pltpu.VMEM((T, HP), F32), pltpu.VMEM((T, HP), F32), pltpu.VMEM((T, HP), F32),
                        pltpu.VMEM((2, T, HP), F32), pltpu.VMEM((2, T, HP), F32), pltpu.VMEM((T, HP), F32)],
        compiler_params=_cparams(("arbitrary", "arbitrary")),
        name="gdn_ctx" if has_ctx else "gdn_prompt",
    )(*args)


def _mix_kernel(x_ref, oa_ref, ob_ref, oc_ref, ga_ref, gb_ref, gc_ref, wout_ref, gt1_ref, n2w_ref,
                sh2_ref, sc2_ref, rw_ref, rb_ref,
                x1_ref, h2_ref, ei_ref, rk_ref, gw_ref, cnt_ref, carry_s):
    i = pl.program_id(0)

    @pl.when(i == 0)
    def _():
        carry_s[...] = jnp.zeros_like(carry_s)

    m = (_sigmoid(ga_ref[...]) * oa_ref[...] + _sigmoid(gb_ref[...]) * ob_ref[...]
         + _sigmoid(gc_ref[...]) * oc_ref[...])
    x1 = x_ref[...] + gt1_ref[...] * jnp.dot(m.astype(BF16), wout_ref[...], preferred_element_type=F32)
    x1_ref[...] = x1
    h2 = _rms(x1) * n2w_ref[...]
    h2 = h2 * (1.0 + sc2_ref[...]) + sh2_ref[...]
    h2_ref[...] = h2
    logits = _dot_hi(h2, rw_ref[...]) + rb_ref[...]

    lane = lax.broadcasted_iota(jnp.int32, (ROWS, N_EXP), 1).astype(F32)
    vals, idxs, sels = [], [], []
    cur = logits
    for _ in range(TOP_K):
        mx = jnp.max(cur, axis=-1, keepdims=True)
        idx = jnp.min(jnp.where(cur == mx, lane, float(N_EXP)), axis=-1, keepdims=True)
        sel = lane == idx
        vals.append(mx)
        idxs.append(idx)
        sels.append(sel)
        cur = jnp.where(sel, -jnp.inf, cur)
    es = [jnp.exp(v - vals[0]) for v in vals]
    den = es[0] + es[1] + es[2] + es[3]
    onehot = jnp.zeros((ROWS, N_EXP), F32)
    for sel in sels:
        onehot = onehot + sel.astype(F32)
    ri = lax.broadcasted_iota(jnp.int32, (ROWS, ROWS), 0)
    ci = lax.broadcasted_iota(jnp.int32, (ROWS, ROWS), 1)
    tri = (ri > ci).astype(BF16)
    rank = jnp.dot(tri, onehot.astype(BF16), preferred_element_type=F32) + carry_s[...]
    l4 = lax.broadcasted_iota(jnp.int32, (1, TOP_K), 1)
    ei = jnp.zeros((ROWS, TOP_K), F32)
    rk = jnp.zeros((ROWS, TOP_K), F32)
    gw = jnp.zeros((ROWS, TOP_K), F32)
    for k in range(TOP_K):
        pick = (l4 == k).astype(F32)
        ei = ei + idxs[k] * pick
        rk = rk + jnp.sum(jnp.where(sels[k], rank, 0.0), axis=-1, keepdims=True) * pick
        gw = gw + (es[k] / den) * pick
    ei_ref[...] = ei.astype(jnp.int32)
    rk_ref[...] = rk.astype(jnp.int32)
    gw_ref[...] = gw
    carry_s[...] = carry_s[...] + jnp.sum(onehot, axis=0, keepdims=True)
    cnt_ref[...] = carry_s[...]


def _mix(x, oa, ob, oc, P, mod, l, w):
    row = pl.BlockSpec((ROWS, D), lambda i: (i, 0))
    full = lambda shape: pl.BlockSpec(shape, lambda i: (0,) * len(shape))
    gate = lambda j: pl.BlockSpec((ROWS, D), lambda i: (i, j))
    tk = pl.BlockSpec((ROWS, TOP_K), lambda i: (i, 0))
    return pl.pallas_call(
        _mix_kernel,
        grid=(N_RB,),
        in_specs=[row, row, row, row, gate(0), gate(1), gate(2), full((D, D)),
                  _mod_spec(2), full((1, D)), _mod_spec(3), _mod_spec(4),
                  full((D, N_EXP)), full((1, N_EXP))],
        out_specs=[row, row, tk, tk, tk, full((1, N_EXP))],
        out_shape=[jax.ShapeDtypeStruct((N_TOK, D), F32), jax.ShapeDtypeStruct((N_TOK, D), F32),
                   jax.ShapeDtypeStruct((N_TOK, TOP_K), jnp.int32),
                   jax.ShapeDtypeStruct((N_TOK, TOP_K), jnp.int32),
                   jax.ShapeDtypeStruct((N_TOK, TOP_K), F32),
                   jax.ShapeDtypeStruct((1, N_EXP), F32)],
        scratch_shapes=[pltpu.VMEM((1, N_EXP), F32)],
        compiler_params=_cparams(("arbitrary",)),
        name="mix_router",
    )(x, oa, ob, oc, P, P, P, w['wout'], mod, w['n2w'], mod, mod, w['rw'], w['rb'])


def _row_copy_wait(src_ref, dst_ref, sem, count):
    def wbody(t, carry):
        pltpu.make_async_copy(src_ref.at[pl.ds(0, 1)], dst_ref.at[pl.ds(0, 1)], sem).wait()
        return carry
    lax.fori_loop(0, count, wbody, 0)


def _scatter_kernel(dest_ref, h_ref, xs_in_ref, xs_ref, sem):
    del xs_in_ref

    def body(t, carry):
        for k in range(TOP_K):
            d = dest_ref[TOP_K * t + k]
            pltpu.make_async_copy(h_ref.at[pl.ds(t, 1)], xs_ref.at[pl.ds(d, 1)], sem).start()
        return carry

    lax.fori_loop(0, ROWS, body, 0)
    _row_copy_wait(h_ref, xs_ref, sem, ROWS * TOP_K)


def _scatter(dest_flat, h2, xs_init):
    return pl.pallas_call(
        _scatter_kernel,
        grid=(N_RB,),
        in_specs=[pl.BlockSpec((ROWS * TOP_K,), lambda i: (i,), memory_space=pltpu.SMEM),
                  pl.BlockSpec((ROWS, D), lambda i: (i, 0)),
                  pl.BlockSpec(memory_space=pl.ANY)],
        out_specs=pl.BlockSpec(memory_space=pl.ANY),
        out_shape=jax.ShapeDtypeStruct((P_ROWS, D), F32),
        scratch_shapes=[pltpu.SemaphoreType.DMA(())],
        input_output_aliases={2: 0},
        compiler_params=_cparams(("arbitrary",)),
        name="moe_scatter",
    )(dest_flat, h2, xs_init)


def _moe_kernel(be_ref, nu_ref, xs_ref, wgu_ref, bgu_ref, wdn_ref, bdn_ref, yb_ref, wgu_s, wdn_s):
    i = pl.program_id(0)
    changed = jnp.logical_or(i == 0, be_ref[i] != be_ref[jnp.maximum(i - 1, 0)])

    @pl.when(changed)
    def _():
        wgu_s[...] = wgu_ref[...].astype(BF16)
        wdn_s[...] = wdn_ref[...].astype(BF16)

    @pl.when(i < nu_ref[0])
    def _():
        gu = jnp.dot(xs_ref[...].astype(BF16), wgu_s[...], preferred_element_type=F32) + bgu_ref[...]
        glu = jnp.minimum(gu[:, 0:FF], LIMIT)
        lin = jnp.clip(gu[:, FF:2 * FF], -LIMIT, LIMIT)
        act = glu * _sigmoid(ALPHA * glu) * (lin + 1.0)
        yb_ref[...] = jnp.dot(act.astype(BF16), wdn_s[...], preferred_element_type=F32) + bdn_ref[...]

    @pl.when(i >= nu_ref[0])
    def _():
        yb_ref[...] = jnp.zeros_like(yb_ref)


def _moe(block_e, n_used, xs, l, w_gu, b_gu, w_dn, b_dn):
    grid_spec = pltpu.PrefetchScalarGridSpec(
        num_scalar_prefetch=2,
        grid=(NB,),
        in_specs=[pl.BlockSpec((BM, D), lambda i, be, nu: (jnp.minimum(i, nu[0] - 1), 0)),
                  pl.BlockSpec((None, None, D, 2 * FF), lambda i, be, nu: (l, be[i], 0, 0)),
                  pl.BlockSpec((None, None, 1, 2 * FF), lambda i, be, nu: (l, be[i], 0, 0)),
                  pl.BlockSpec((None, None, FF, D), lambda i, be, nu: (l, be[i], 0, 0)),
                  pl.BlockSpec((None, None, 1, D), lambda i, be, nu: (l, be[i], 0, 0))],
        out_specs=pl.BlockSpec((BM, D), lambda i, be, nu: (i, 0)),
        scratch_shapes=[pltpu.VMEM((D, 2 * FF), BF16), pltpu.VMEM((FF, D), BF16)])
    return pl.pallas_call(
        _moe_kernel,
        grid_spec=grid_spec,
        out_shape=jax.ShapeDtypeStruct((P_ROWS, D), F32),
        compiler_params=_cparams(("arbitrary",)),
        name="moe_experts",
    )(block_e, n_used, xs, w_gu, b_gu.reshape(DEPTH, N_EXP, 1, 2 * FF), w_dn,
      b_dn.reshape(DEPTH, N_EXP, 1, D))


def _make_combine_kernel(final):
    def kern(*refs):
        if final:
            dest_ref, x1_ref, gw_ref, gt2_ref, fw_ref, yb_ref, o_ref, buf, sem = refs
        else:
            dest_ref, x1_ref, gw_ref, gt2_ref, yb_ref, o_ref, buf, sem = refs

        def body(t, carry):
            for k in range(TOP_K):
                d = dest_ref[TOP_K * t + k]
                pltpu.make_async_copy(yb_ref.at[pl.ds(d, 1)], buf.at[k, pl.ds(t, 1)], sem).start()
            return carry

        lax.fori_loop(0, ROWS, body, 0)
        _row_copy_wait(yb_ref, buf.at[0], sem, ROWS * TOP_K)
        gw = gw_ref[...]
        y = gw[:, 0:1] * buf[0]
        for k in range(1, TOP_K):
            y = y + gw[:, k:k + 1] * buf[k]
        x2 = x1_ref[...] + gt2_ref[...] * y
        if final:
            x2 = _rms(x2) * fw_ref[...]
        o_ref[...] = x2

    return kern


def _combine(dest_flat, x1, gw, mod, yb, final_w):
    final = final_w is not None
    in_specs = [pl.BlockSpec((ROWS * TOP_K,), lambda i: (i,), memory_space=pltpu.SMEM),
                pl.BlockSpec((ROWS, D), lambda i: (i, 0)),
                pl.BlockSpec((ROWS, TOP_K), lambda i: (i, 0)),
                _mod_spec(5)]
    args = [dest_flat, x1, gw, mod]
    if final:
        in_specs.append(pl.BlockSpec((1, D), lambda i: (0, 0)))
        args.append(final_w)
    in_specs.append(pl.BlockSpec(memory_space=pl.ANY))
    args.append(yb)
    return pl.pallas_call(
        _make_combine_kernel(final),
        grid=(N_RB,),
        in_specs=in_specs,
        out_specs=pl.BlockSpec((ROWS, D), lambda i: (i, 0)),
        out_shape=jax.ShapeDtypeStruct((N_TOK, D), F32),
        scratch_shapes=[pltpu.VMEM((TOP_K, ROWS, D), F32), pltpu.SemaphoreType.DMA(())],
        compiler_params=_cparams(("arbitrary",)),
        name="moe_combine_final" if final else "moe_combine",
    )(*args)


def _rope_tables():
    rows = DEC_SEQ // GRID_W
    row = np.repeat(np.arange(rows, dtype=np.float32), GRID_W)
    col = np.tile(np.arange(GRID_W, dtype=np.float32), rows)
    inv = (ROPE_THETA ** (-np.arange(ROPE_PAIRS, dtype=np.float32) / ROPE_PAIRS)).astype(np.float32)
    ar = (row[:, None] * inv).astype(np.float32)
    ac = (col[:, None] * inv).astype(np.float32)
    cr, sr, cc, sc = np.cos(ar), np.sin(ar), np.cos(ac), np.sin(ac)
    cos32 = np.concatenate([cr, cr, cc, cc], axis=1).astype(np.float32)
    sin32 = np.concatenate([-sr, sr, -sc, sc], axis=1).astype(np.float32)
    return (jnp.asarray(np.tile(cos32, (1, H_MLA))), jnp.asarray(np.tile(sin32, (1, H_MLA))),
            jnp.asarray(cos32), jnp.asarray(sin32))


def _dft_tables(T):
    c = np.arange(FNO_C)
    ang_c = 2.0 * np.pi * ((c[:, None] * c[None, :]) % FNO_C) / FNO_C
    eye = np.eye(FNO_G)
    cc = np.kron(eye, np.cos(ang_c)).astype(np.float32)
    sc = np.kron(eye, np.sin(ang_c)).astype(np.float32)
    t = np.arange(T)
    ang_t = 2.0 * np.pi * ((t[:, None] * t[None, :]) % T) / T
    return (jnp.asarray(cc), jnp.asarray(sc), jnp.asarray(np.cos(ang_t).astype(np.float32)),
            jnp.asarray(np.sin(ang_t).astype(np.float32)))


_SWAP32 = np.concatenate([np.arange(8, 16), np.arange(0, 8), np.arange(24, 32), np.arange(16, 24)])


def _pad_heads(a, n_heads, width):
    lead = a.shape[:-1]
    a = a.reshape(lead + (n_heads, width))
    a = jnp.pad(a, [(0, 0)] * len(lead) + [(0, 0), (0, HP - width)])
    return a.reshape(lead + (n_heads * HP,))


def _pack_w_in(w):
    o = np.cumsum([0, Q_RANK, KV_RANK, ROPE, FNO_W, 256, 256, 256, 256, 4, 4, 4, 4, 3 * D])
    seg = lambda j: w[:, o[j]:o[j + 1]]
    kr = seg(2)
    misc = jnp.concatenate([seg(1), kr, kr[:, _SWAP32], seg(8), seg(9), seg(10), seg(11),
                            jnp.zeros((D, 256 - KV_RANK - 2 * ROPE - 16), w.dtype)], axis=1)
    packed = jnp.concatenate([seg(12), seg(0), misc, seg(3),
                              _pad_heads(seg(4), H_GDN, DK), _pad_heads(seg(5), H_GDN, DK),
                              _pad_heads(seg(6), H_GDN, DVG), _pad_heads(seg(7), H_GDN, DVG)], axis=1)
    return packed.astype(BF16)


def _pack_w_uq(w):
    w3 = w.reshape(Q_RANK, H_MLA, NOPE + ROPE)
    nope = w3[:, :, :NOPE].reshape(Q_RANK, H_MLA * NOPE)
    pe = w3[:, :, NOPE:]
    return jnp.concatenate([nope, pe.reshape(Q_RANK, H_MLA * ROPE),
                            pe[:, :, _SWAP32].reshape(Q_RANK, H_MLA * ROPE)], axis=1).astype(BF16)


def _pack_w_ukv(w):
    w3 = w.reshape(KV_RANK, H_MLA, NOPE + DV_MLA)
    return jnp.concatenate([w3[:, :, :NOPE].reshape(KV_RANK, H_MLA * NOPE),
                            w3[:, :, NOPE:].reshape(KV_RANK, H_MLA * DV_MLA)], axis=1).astype(BF16)


def kernel(x_prompt, x_sample, cache_ckv, cache_kpe, state_delta, c, c_ctx, ada_w, ada_b, norm1_w, norm2_w, w_in, mla_qn_w, mla_w_uq, mla_kvn_w, mla_w_ukv, mla_w_o, fno_w_o, gdn_conv_w, gdn_A_log, gdn_dt_bias, gdn_onorm_w, gdn_w_o, w_out, router_w, router_b, moe_w_gu, moe_b_gu, moe_w_dn, moe_b_dn, final_norm_w):
    x = jnp.concatenate([x_prompt.reshape(N_PROMPT, D), x_sample.reshape(N_SAMPLE, D)], axis=0)
    cond8 = jnp.concatenate([c_ctx[None, :], c, jnp.zeros((8 - 1 - DEC_BATCH, D), F32)], axis=0)
    mod_all = _ada(cond8, ada_w, ada_b).reshape(DEPTH, 8, 6, 1, D)
    rope = _rope_tables()
    dft_p, dft_s = _dft_tables(SEQ), _dft_tables(DEC_SEQ)
    s0_pad = jnp.pad(state_delta, [(0, 0)] * 4 + [(0, HP - DK), (0, HP - DVG)])
    xs_zero = jnp.zeros((P_ROWS, D), F32)
    rb_s = N_PROMPT // DEC_SEQ

    ckvs, kpes, states = [], [], []
    for l in range(DEPTH):
        mod = mod_all[l]
        P = _proj_in(x, norm1_w[l][None, :], mod, _pack_w_in(w_in[l]))

        wm = {'qn': mla_qn_w[l][None, :], 'wuq': _pack_w_uq(mla_w_uq[l]), 'kvn': mla_kvn_w[l][None, :],
              'wukv': _pack_w_ukv(mla_w_ukv[l]), 'wo': mla_w_o[l].astype(BF16)}
        oa_p, ckv, kpe = _mla(P, l, SEQ, BATCH, 0, None, wm, None)
        (oa_s,) = _mla(P, l, DEC_SEQ, DEC_BATCH, rb_s, (cache_ckv, cache_kpe), wm, rope)
        ckvs.append(ckv)
        kpes.append(kpe)

        fwo = fno_w_o[l].astype(BF16)
        ob_p = _fnet(P, SEQ, BATCH, 0, dft_p, fwo)
        ob_s = _fnet(P, DEC_SEQ, DEC_BATCH, rb_s, dft_s, fwo)

        cw = gdn_conv_w[l]
        cw = jnp.stack([_pad_heads(cw[:, 256 * s:256 * (s + 1)], H_GDN, DK) for s in range(3)], axis=0)
        cw = cw.reshape(3, CONV, H_GDN, HP).transpose(2, 0, 1, 3).reshape(H_GDN, 3 * CONV, HP)
        par = jnp.concatenate([gdn_A_log[l].T, gdn_dt_bias[l].T], axis=1)[:, None, :]
        onw = jnp.pad(gdn_onorm_w[l], (0, HP - DVG))[None, :]
        gwo = jnp.pad(gdn_w_o[l].reshape(H_GDN, DVG, D), [(0, 0), (0, HP - DVG), (0, 0)]).astype(BF16)
        wg = {'cw': cw, 'par': par, 'onw': onw, 'wo': gwo}
        small = P[:, OFF_MISC + 192:OFF_MISC + 208].reshape(N_TOK, 4, H_GDN)
        bcol = small.transpose(2, 0, 1)
        brow = bcol.reshape(H_GDN, N_TOK // CHUNK, CHUNK, 4).transpose(0, 1, 3, 2)
        oc_p, st = _gdn(P, l, SEQ, BATCH, 0, None, wg, bcol, brow)
        (oc_s,) = _gdn(P, l, DEC_SEQ, DEC_BATCH, rb_s, s0_pad, wg, bcol, brow)
        states.append(st)

        oa = jnp.concatenate([oa_p, oa_s], axis=0)
        ob = jnp.concatenate([ob_p, ob_s], axis=0)
        oc = jnp.concatenate([oc_p, oc_s], axis=0)
        wx = {'wout': w_out[l].astype(BF16), 'n2w': norm2_w[l][None, :], 'rw': router_w[l],
              'rb': router_b[l][None, :]}
        x1, h2, ei, rk, gw, cnt = _mix(x, oa, ob, oc, P, mod, l, wx)

        counts = cnt[0].astype(jnp.int32)
        padded = (counts + BM - 1) // BM * BM
        pad_end = jnp.cumsum(padded)
        pad_start = pad_end - padded
        dest = (jnp.take(pad_start, ei, axis=0) + rk).reshape(NK)
        n_used = (pad_end[-1] // BM).astype(jnp.int32)
        blk = jnp.arange(NB, dtype=jnp.int32)
        block_e = jnp.minimum(jnp.searchsorted(pad_end, blk * BM, side='right'), N_EXP - 1).astype(jnp.int32)
        block_e = jnp.where(blk < n_used, block_e, block_e[n_used - 1])

        xs = _scatter(dest, h2, xs_zero)
        yb = _moe(block_e, n_used[None], xs, l, moe_w_gu, moe_b_gu, moe_w_dn, moe_b_dn)
        x = _combine(dest, x1, gw, mod, yb, final_norm_w[None, :] if l == DEPTH - 1 else None)

    y_prompt = x[:N_PROMPT].reshape(BATCH, SEQ, D)
    y_sample = x[N_PROMPT:].reshape(DEC_BATCH, DEC_SEQ, D)
    return (y_prompt, y_sample, jnp.stack(ckvs, axis=1), jnp.stack(kpes, axis=1), jnp.stack(states, axis=1))
```

```python
import functools

import numpy as np
import jax
import jax.numpy as jnp
from jax import lax
from jax.experimental import pallas as pl
from jax.experimental.pallas import tpu as pltpu

F32 = jnp.float32
BF16 = jnp.bfloat16
HI = lax.Precision.HIGHEST

D = 1024
BATCH, SEQ = 32, 256
DEC_BATCH, DEC_SEQ = 2, 1024
DEPTH = 2
PAST = 512
GRID_W = 64
EPS = 1e-6
N_PROMPT = BATCH * SEQ
N_SAMPLE = DEC_BATCH * DEC_SEQ
N_TOK = N_PROMPT + N_SAMPLE
ROWS = 256
N_RB = N_TOK // ROWS
N_RB_PROMPT = N_PROMPT // ROWS
RB_PER_SAMPLE = DEC_SEQ // ROWS

H_MLA, Q_RANK, KV_RANK, NOPE, ROPE, DV_MLA = 8, 256, 128, 64, 32, 64
ROPE_PAIRS = ROPE // 4
ROPE_THETA = 10000.0
QB = 256
FNO_G, FNO_C = 4, 64
FNO_W = FNO_G * FNO_C
H_GDN, DK, DVG, CONV, CHUNK = 4, 64, 64, 3, 64
HP = 128
CPB = 4
N_EXP, TOP_K, FF = 32, 4, 1024
ALPHA, LIMIT = 1.702, 7.0
BM = 256
NK = N_TOK * TOP_K
NB = NK // BM + N_EXP
P_ROWS = NB * BM

OFF_GATES, OFF_QC, OFF_MISC, OFF_UF = 0, 3072, 3328, 3584
OFF_GQ, OFF_GK, OFF_GV, OFF_GZ = 3840, 4352, 4864, 5376
PC = 5888
PC_HALF = PC // 2
VMEM_LIMIT = 52 * 1024 * 1024


def _cparams(sem):
    return pltpu.CompilerParams(dimension_semantics=sem, vmem_limit_bytes=VMEM_LIMIT)


def _sigmoid(x):
    return 1.0 / (1.0 + jnp.exp(-x))


def _softplus(x):
    return jnp.maximum(x, 0.0) + jnp.log(1.0 + jnp.exp(-jnp.abs(x)))


def _dot(a, b):
    return jnp.dot(a.astype(BF16), b.astype(BF16), preferred_element_type=F32)


def _dot_hi(a, b):
    return jnp.dot(a, b, preferred_element_type=F32, precision=HI)


def _dot_nt(a, b):
    return lax.dot_general(a.astype(BF16), b.astype(BF16), (((1,), (1,)), ((), ())),
                           preferred_element_type=F32)


def _dot_tn(a, b):
    return lax.dot_general(a.astype(BF16), b.astype(BF16), (((0,), (0,)), ((), ())),
                           preferred_element_type=F32)


def _rms(x, n=None):
    n = x.shape[-1] if n is None else n
    return x * lax.rsqrt(jnp.sum(x * x, axis=-1, keepdims=True) * (1.0 / n) + EPS)


def _cond_index(i):
    return jnp.where(i < N_RB_PROMPT, 0, 1 + (i - N_RB_PROMPT) // RB_PER_SAMPLE)


def _ada_kernel(c_ref, w_ref, b_ref, o_ref):
    c = c_ref[...]
    o_ref[...] = _dot_hi(c * _sigmoid(c), w_ref[...]) + b_ref[...]


def _ada(cond8, ada_w, ada_b):
    nc = 4
    cb = 6 * D // nc
    return pl.pallas_call(
        _ada_kernel,
        grid=(DEPTH, nc),
        in_specs=[pl.BlockSpec((8, D), lambda l, j: (0, 0)),
                  pl.BlockSpec((None, D, cb), lambda l, j: (l, 0, j)),
                  pl.BlockSpec((None, 1, cb), lambda l, j: (l, 0, j))],
        out_specs=pl.BlockSpec((None, 8, cb), lambda l, j: (l, 0, j)),
        out_shape=jax.ShapeDtypeStruct((DEPTH, 8, 6 * D), F32),
        compiler_params=_cparams(("arbitrary", "arbitrary")),
        name="ada_mod",
    )(cond8, ada_w, ada_b.reshape(DEPTH, 1, 6 * D))


def _mod_spec(which, two_d=False):
    if two_d:
        return pl.BlockSpec((None, None, 1, D), lambda j, i: (_cond_index(i), which, 0, 0))
    return pl.BlockSpec((None, None, 1, D), lambda i: (_cond_index(i), which, 0, 0))


def _proj_in_kernel(x_ref, nw_ref, sh_ref, sc_ref, w_ref, o_ref):
    h = _rms(x_ref[...]) * nw_ref[...]
    h = h * (1.0 + sc_ref[...]) + sh_ref[...]
    o_ref[...] = jnp.dot(h.astype(BF16), w_ref[...], preferred_element_type=F32)


def _proj_in(x, norm_w, mod, w_packed):
    return pl.pallas_call(
        _proj_in_kernel,
        grid=(2, N_RB),
        in_specs=[pl.BlockSpec((ROWS, D), lambda j, i: (i, 0)),
                  pl.BlockSpec((1, D), lambda j, i: (0, 0)),
                  _mod_spec(0, True), _mod_spec(1, True),
                  pl.BlockSpec((D, PC_HALF), lambda j, i: (0, j))],
        out_specs=pl.BlockSpec((ROWS, PC_HALF), lambda j, i: (i, j)),
        out_shape=jax.ShapeDtypeStruct((N_TOK, PC), F32),
        compiler_params=_cparams(("arbitrary", "arbitrary")),
        name="proj_in",
    )(x, norm_w, mod, mod, w_packed)


def _make_mla_kernel(T, has_ctx):
    scale = (NOPE + ROPE) ** -0.5

    def kern(*refs):
        if has_ctx:
            (qc_ref, misc_ref, cckv_ref, ckpe_ref, qnw_ref, wuq_ref, kvnw_ref, wukv_ref, wo_ref,
             cq_ref, sq_ref, ck_ref, sk_ref, o_ref) = refs
        else:
            (qc_ref, misc_ref, qnw_ref, wuq_ref, kvnw_ref, wukv_ref, wo_ref,
             o_ref, ckv_out, kpe_out) = refs
        qn = _rms(qc_ref[...]) * qnw_ref[...]
        qa = jnp.dot(qn.astype(BF16), wuq_ref[...], preferred_element_type=F32)
        misc = misc_ref[...]
        ckv = _rms(misc[:, 0:KV_RANK]) * kvnw_ref[...]
        kr = misc[:, KV_RANK:KV_RANK + ROPE]
        q_pe = qa[:, 512:768]
        if has_ctx:
            q_pe = q_pe * cq_ref[...] + qa[:, 768:1024] * sq_ref[...]
            kr = kr * ck_ref[...] + misc[:, KV_RANK + ROPE:KV_RANK + 2 * ROPE] * sk_ref[...]
            keys_ckv = jnp.concatenate([cckv_ref[...], ckv], axis=0)
            keys_pe = jnp.concatenate([ckpe_ref[...], kr], axis=0)
        else:
            keys_ckv, keys_pe = ckv, kr
            ckv_out[...] = ckv
            kpe_out[...] = kr
        kv = jnp.dot(keys_ckv.astype(BF16), wukv_ref[...], preferred_element_type=F32)
        q_nope = qa[:, 0:512].astype(BF16)
        q_pe = q_pe.astype(BF16)
        kpe_b = keys_pe.astype(BF16)
        kv_b = kv.astype(BF16)
        for qb in range(T // QB):
            r = slice(qb * QB, (qb + 1) * QB)
            acc = jnp.zeros((QB, D), F32)
            for h in range(H_MLA):
                s = (_dot_nt(q_nope[r, NOPE * h:NOPE * (h + 1)], kv_b[:, NOPE * h:NOPE * (h + 1)])
                     + _dot_nt(q_pe[r, ROPE * h:ROPE * (h + 1)], kpe_b)) * scale
                p = jnp.exp(s - jnp.max(s, axis=-1, keepdims=True))
                den = jnp.sum(p, axis=-1, keepdims=True)
                o_h = _dot(p, kv_b[:, 512 + DV_MLA * h:512 + DV_MLA * (h + 1)]) / den
                acc = acc + jnp.dot(o_h.astype(BF16), wo_ref[DV_MLA * h:DV_MLA * (h + 1), :],
                                    preferred_element_type=F32)
            o_ref[r, :] = acc

    return kern


def _mla(P, l, T, nseq, rb0, ctx, w, rope):
    has_ctx = ctx is not None
    qc_blk, misc_blk = OFF_QC // 256, OFF_MISC // 256
    full = lambda shape: pl.BlockSpec(shape, lambda b: (0,) * len(shape))
    in_specs = [pl.BlockSpec((T, 256), lambda b: (rb0 + b, qc_blk)),
                pl.BlockSpec((T, 256), lambda b: (rb0 + b, misc_blk))]
    args = [P, P]
    if has_ctx:
        in_specs += [pl.BlockSpec((None, None, PAST, KV_RANK), lambda b: (b, l, 0, 0)),
                     pl.BlockSpec((None, None, PAST, ROPE), lambda b: (b, l, 0, 0))]
        args += [ctx[0], ctx[1]]
    in_specs += [full((1, Q_RANK)), full((Q_RANK, 1024)), full((1, KV_RANK)), full((KV_RANK, 1024)),
                 full((512, D))]
    args += [w['qn'], w['wuq'], w['kvn'], w['wukv'], w['wo']]
    out_shape = [jax.ShapeDtypeStruct((nseq * T, D), F32)]
    out_specs = [pl.BlockSpec((T, D), lambda b: (b, 0))]
    if has_ctx:
        in_specs += [full((T, 256)), full((T, 256)), full((T, ROPE)), full((T, ROPE))]
        args += list(rope)
    else:
        out_shape += [jax.ShapeDtypeStruct((nseq, T, KV_RANK), F32),
                      jax.ShapeDtypeStruct((nseq, T, ROPE), F32)]
        out_specs += [pl.BlockSpec((None, T, KV_RANK), lambda b: (b, 0, 0)),
                      pl.BlockSpec((None, T, ROPE), lambda b: (b, 0, 0))]
    return pl.pallas_call(
        _make_mla_kernel(T, has_ctx),
        grid=(nseq,),
        in_specs=in_specs, out_specs=out_specs, out_shape=out_shape,
        compiler_params=_cparams(("arbitrary",)),
        name="mla_ctx" if has_ctx else "mla_prompt",
    )(*args)


def _make_fnet_kernel(T):
    scale = 1.0 / float(np.sqrt(T * FNO_C))

    def kern(u_ref, cc_ref, sc_ref, ct_ref, st_ref, wo_ref, o_ref):
        u = u_ref[...]
        a = _dot_hi(u, cc_ref[...])
        b = _dot_hi(u, sc_ref[...])
        f = (_dot_hi(ct_ref[...], a) - _dot_hi(st_ref[...], b)) * scale
        o_ref[...] = jnp.dot(f.astype(BF16), wo_ref[...], preferred_element_type=F32)

    return kern


def _fnet(P, T, nseq, rb0, tabs, wo):
    full = lambda shape: pl.BlockSpec(shape, lambda b: (0,) * len(shape))
    return pl.pallas_call(
        _make_fnet_kernel(T),
        grid=(nseq,),
        in_specs=[pl.BlockSpec((T, FNO_W), lambda b: (rb0 + b, OFF_UF // 256)),
                  full((FNO_W, FNO_W)), full((FNO_W, FNO_W)), full((T, T)), full((T, T)),
                  full((FNO_W, D))],
        out_specs=pl.BlockSpec((T, D), lambda b: (b, 0)),
        out_shape=jax.ShapeDtypeStruct((nseq * T, D), F32),
        compiler_params=_cparams(("arbitrary",)),
        name=f"fnet_{T}",
    )(P, tabs[0], tabs[1], tabs[2], tabs[3], wo)


def _make_gdn_kernel(T, has_ctx):
    n = T // CHUNK
    C = CHUNK

    def kern(*refs):
        if has_ctx:
            (q_ref, k_ref, v_ref, z_ref, cw_ref, bcol_ref, brow_ref, par_ref, onw_ref, wo_ref, s0_ref,
             oc_ref, qh_s, kh_s, vh_s, g_s, b_s, o_s) = refs
        else:
            (q_ref, k_ref, v_ref, z_ref, cw_ref, bcol_ref, brow_ref, par_ref, onw_ref, wo_ref,
             oc_ref, sfin_ref, qh_s, kh_s, vh_s, g_s, b_s, o_s) = refs
        h = pl.program_id(1)
        rows = lax.broadcasted_iota(jnp.int32, (T, 1), 0)
        cw = cw_ref[...]

        def conv(x_ref, seg):
            x = x_ref[...]
            xp = jnp.where(rows == 0, 0.0, pltpu.roll(x, 1, 0))
            xn = jnp.where(rows == T - 1, 0.0, pltpu.roll(x, T - 1, 0))
            y = (xp * cw[3 * seg:3 * seg + 1] + x * cw[3 * seg + 1:3 * seg + 2]
                 + xn * cw[3 * seg + 2:3 * seg + 3])
            return y * _sigmoid(y)

        yq = conv(q_ref, 0)
        qh_s[...] = yq * lax.rsqrt(jnp.sum(yq * yq, axis=-1, keepdims=True) + EPS) * (DK ** -0.5)
        yk = conv(k_ref, 1)
        kh_s[...] = yk * lax.rsqrt(jnp.sum(yk * yk, axis=-1, keepdims=True) + EPS)
        vh_s[...] = conv(v_ref, 2)

        par = par_ref[...]
        bc = bcol_ref[...]
        ones = jnp.ones((1, HP), F32)
        for d in range(2):
            g = -jnp.exp(par[:, d:d + 1]) * _softplus(bc[:, 2 + d:3 + d] + par[:, 2 + d:3 + d])
            g_s[d] = g * ones
            b_s[d] = _sigmoid(bc[:, d:d + 1]) * ones

        ii = lax.broadcasted_iota(jnp.int32, (C, C), 0)
        jj = lax.broadcasted_iota(jnp.int32, (C, C), 1)
        lower = (ii >= jj).astype(F32)
        upper = (ii <= jj).astype(F32)
        eye = (ii == jj).astype(F32)

        def chunk_step(d, c, S):
            r0 = pl.multiple_of(c * C, C)
            q = qh_s[pl.ds(r0, C), :]
            k = kh_s[pl.ds(r0, C), :]
            v = vh_s[pl.ds(r0, C), :]
            gcol = g_s[d, pl.ds(r0, C), :]
            beta = b_s[d, pl.ds(r0, C), :]
            br = brow_ref[c]
            grow = -jnp.exp(par[:, d:d + 1]) * _softplus(br[2 + d:3 + d, :] + par[:, 2 + d:3 + d])
            grow8 = jnp.broadcast_to(grow, (8, C))
            if d == 0:
                gc = _dot_hi(lower, gcol)
                gr = _dot_hi(grow8, upper)[0:1, :]
                incl, strict = ii >= jj, ii > jj
            else:
                gc = _dot_hi(upper, gcol)
                gr = _dot_hi(grow8, lower)[0:1, :]
                incl, strict = ii <= jj, ii < jj
            diff = gc[:, 0:C] - gr
            decay = jnp.where(incl, jnp.exp(jnp.where(incl, diff, 0.0)), 0.0)
            kb = k * beta
            vb = v * beta
            a = jnp.where(strict, _dot_nt(kb, k) * decay, 0.0)
            x = eye - a
            p = _dot_hi(a, a)
            for it in range(5):
                x = x + _dot_hi(x, p)
                if it < 4:
                    p = _dot_hi(p, p)
            eg = jnp.exp(gc)
            u = _dot_hi(x, vb)
            w = _dot_hi(x, kb * eg)
            qk = _dot_nt(q, k) * decay
            v_new = u - _dot(w, S)
            o = _dot(q * eg, S) + _dot(qk, v_new)
            glast = gc[C - 1:C, :] if d == 0 else gc[0:1, :]
            S = S * jnp.exp(glast) + _dot_tn(k * jnp.exp(glast - gc), v_new)
            if d == 0:
                o_s[pl.ds(r0, C), :] = o
            else:
                o_s[pl.ds(r0, C), :] = o_s[pl.ds(r0, C), :] + o
            return S

        finals = []
        for d in range(2):
            S0 = s0_ref[d] if has_ctx else jnp.zeros((HP, HP), F32)
            if d == 0:
                S = lax.fori_loop(0, n, lambda c, S: chunk_step(0, c, S), S0)
            else:
                S = lax.fori_loop(0, n, lambda c, S: chunk_step(1, n - 1 - c, S), S0)
            finals.append(S)
        if not has_ctx:
            sfin_ref[0] = finals[0][0:DK, 0:DVG]
            sfin_ref[1] = finals[1][0:DK, 0:DVG]

        o = o_s[...]
        z = z_ref[...]
        y = _rms(o, DVG) * onw_ref[...] * (z * _sigmoid(z))
        val = jnp.dot(y.astype(BF16), wo_ref[...], preferred_element_type=F32)

        @pl.when(h == 0)
        def _():
            oc_ref[...] = val

        @pl.when(h != 0)
        def _():
            oc_ref[...] = oc_ref[...] + val

    return kern


def _split_lhs(x):
    hi = x.astype(BF16)
    lo = (x - hi.astype(F32)).astype(BF16)
    return jnp.concatenate([hi, lo], axis=1)


def _split_rhs(p):
    hi = p.astype(BF16)
    lo = (p - hi.astype(F32)).astype(BF16)
    return jnp.concatenate([hi, lo, hi, lo], axis=0)


def _split3_rows(g):
    g1 = g.astype(BF16)
    r = g - g1.astype(F32)
    g2 = r.astype(BF16)
    g3 = (r - g2.astype(F32)).astype(BF16)
    return jnp.concatenate([g1, g2, g3, jnp.zeros_like(g1)], axis=0)


def _make_gdn_kernel2(T, has_ctx):
    n = T // CHUNK
    C = CHUNK
    bf = lambda t: t.astype(BF16)

    def kern(*refs):
        if has_ctx:
            (q_ref, k_ref, v_ref, z_ref, cw_ref, bcol_ref, par_ref, onw_ref, wo_ref, s0_ref,
             oc_ref, qh_s, kh_s, vh_s, g_s, b_s, u_s, w_s, qk_s, qe_s, kd_s, egl_s, o_s) = refs
        else:
            (q_ref, k_ref, v_ref, z_ref, cw_ref, bcol_ref, par_ref, onw_ref, wo_ref,
             oc_ref, sfin_ref, qh_s, kh_s, vh_s, g_s, b_s, u_s, w_s, qk_s, qe_s, kd_s, egl_s, o_s) = refs
        h = pl.program_id(1)
        rows = lax.broadcasted_iota(jnp.int32, (T, 1), 0)
        cw = cw_ref[...]

        def conv(x_ref, seg):
            x = x_ref[...]
            xp = jnp.where(rows == 0, 0.0, pltpu.roll(x, 1, 0))
            xn = jnp.where(rows == T - 1, 0.0, pltpu.roll(x, T - 1, 0))
            y = (xp * cw[3 * seg:3 * seg + 1] + x * cw[3 * seg + 1:3 * seg + 2]
                 + xn * cw[3 * seg + 2:3 * seg + 3])
            return y * _sigmoid(y)

        yq = conv(q_ref, 0)
        qh_s[...] = yq * lax.rsqrt(jnp.sum(yq * yq, axis=-1, keepdims=True) + EPS) * (DK ** -0.5)
        yk = conv(k_ref, 1)
        kh_s[...] = yk * lax.rsqrt(jnp.sum(yk * yk, axis=-1, keepdims=True) + EPS)
        vh_s[...] = conv(v_ref, 2)

        par = par_ref[...]
        bc = bcol_ref[...]
        ones = jnp.ones((1, HP), F32)
        for d in range(2):
            g = -jnp.exp(par[:, d:d + 1]) * _softplus(bc[:, 2 + d:3 + d] + par[:, 2 + d:3 + d])
            g_s[d] = g * ones
            b_s[d] = _sigmoid(bc[:, d:d + 1]) * ones

        ii = lax.broadcasted_iota(jnp.int32, (C, HP), 0)
        jj = lax.broadcasted_iota(jnp.int32, (C, HP), 1) % C
        eye = (ii == jj).astype(F32)
        masks = ((ii >= jj, ii > jj), (ii <= jj, ii < jj))
        ri = lax.broadcasted_iota(jnp.int32, (C, 2 * HP), 0)
        li = lax.broadcasted_iota(jnp.int32, (C, 2 * HP), 1)
        lc = li % C
        lv = li < 3 * C
        cum_l = (bf(jnp.logical_and(ri >= lc, lv).astype(F32)), bf(jnp.logical_and(ri <= lc, lv).astype(F32)))
        r4 = lax.broadcasted_iota(jnp.int32, (4 * C, HP), 0)
        c4 = lax.broadcasted_iota(jnp.int32, (4 * C, HP), 1) % C
        rm = r4 % C
        rv = r4 < 3 * C
        cum_r = (bf(jnp.logical_and(rm <= c4, rv).astype(F32)), bf(jnp.logical_and(rm >= c4, rv).astype(F32)))

        def prep(it, carry):
            st = []
            for j in range(CPB):
                c = it * CPB + j
                r0 = pl.multiple_of(c * C, C)
                q = qh_s[pl.ds(r0, C), :]
                k = kh_s[pl.ds(r0, C), :]
                v = vh_s[pl.ds(r0, C), :]
                for d in range(2):
                    st.append(dict(d=d, c=c, r0=r0, q=q, k=k, v=v, g=g_s[d, pl.ds(r0, C), :],
                                   beta=b_s[d, pl.ds(r0, C), :]))
            for s in st:
                g3 = _split3_rows(s['g'])
                s['gc'] = jnp.dot(cum_l[s['d']], g3, preferred_element_type=F32)
                s['gr'] = lax.dot_general(g3, cum_r[s['d']], (((0,), (0,)), ((), ())),
                                          preferred_element_type=F32)[0:C, :]
                s['kb'] = s['k'] * s['beta']
                s['gram'] = lax.dot_general(bf(jnp.concatenate([s['kb'], s['q']], axis=0)),
                                            bf(jnp.concatenate([s['k'], s['k']], axis=0)),
                                            (((1,), (1,)), ((), ())), preferred_element_type=F32)
            for s in st:
                incl, strict = masks[s['d']]
                decay = jnp.where(incl, jnp.exp(jnp.where(incl, s['gc'] - s['gr'], 0.0)), 0.0)
                a = jnp.where(strict, s['gram'][0:C] * decay, 0.0)
                s['qk'] = s['gram'][C:2 * C] * decay
                s['x'] = eye - a
                s['p'] = jnp.dot(_split_lhs(a), _split_rhs(a), preferred_element_type=F32)
            for lvl in range(5):
                for s in st:
                    if lvl < 4:
                        r = jnp.dot(_split_lhs(jnp.concatenate([s['x'], s['p']], axis=0)), _split_rhs(s['p']),
                                    preferred_element_type=F32)
                        s['x'] = s['x'] + r[0:C]
                        s['p'] = r[C:2 * C]
                    else:
                        s['x'] = s['x'] + jnp.dot(_split_lhs(s['x']), _split_rhs(s['p']),
                                                  preferred_element_type=F32)
            for s in st:
                eg = jnp.exp(s['gc'])
                s['uw'] = jnp.dot(_split_lhs(s['x']),
                                  _split_rhs(jnp.concatenate([s['v'] * s['beta'], s['kb'] * eg], axis=1)),
                                  preferred_element_type=F32)
                glast = s['gc'][C - 1:C, :] if s['d'] == 0 else s['gc'][0:1, :]
                s['qe'] = s['q'] * eg
                s['kd'] = s['k'] * jnp.exp(glast - s['gc'])
                s['egl'] = jnp.broadcast_to(jnp.exp(glast), (8, HP))
            for s in st:
                d, r0 = s['d'], s['r0']
                u_s[d, pl.ds(r0, C), :] = s['uw'][:, 0:HP]
                w_s[d, pl.ds(r0, C), :] = s['uw'][:, HP:2 * HP]
                qk_s[d, pl.ds(r0, C), :] = s['qk']
                qe_s[d, pl.ds(r0, C), :] = s['qe']
                kd_s[d, pl.ds(r0, C), :] = s['kd']
                egl_s[d, s['c']] = s['egl']
            return carry

        lax.fori_loop(0, n // CPB, prep, 0)

        def scan(i, Ss):
            loaded = []
            for d in range(2):
                c = i if d == 0 else n - 1 - i
                r0 = pl.multiple_of(c * C, C)
                loaded.append((r0, w_s[d, pl.ds(r0, C), :], qe_s[d, pl.ds(r0, C), :], u_s[d, pl.ds(r0, C), :],
                               qk_s[d, pl.ds(r0, C), 0:C], kd_s[d, pl.ds(r0, C), :], egl_s[d, c]))
            rs = [jnp.dot(bf(jnp.concatenate([loaded[d][1], loaded[d][2]], axis=0)), bf(Ss[d]),
                          preferred_element_type=F32) for d in range(2)]
            v_new = [loaded[d][3] - rs[d][0:C] for d in range(2)]
            os_ = [rs[d][C:2 * C] + jnp.dot(bf(loaded[d][4]), bf(v_new[d]), preferred_element_type=F32)
                   for d in range(2)]
            out = [Ss[d] * loaded[d][6][0:1, :] + lax.dot_general(bf(loaded[d][5]), bf(v_new[d]),
                                                                 (((0,), (0,)), ((), ())),
                                                                 preferred_element_type=F32) for d in range(2)]
            for d in range(2):
                o_s[d, pl.ds(loaded[d][0], C), :] = os_[d]
            return tuple(out)

        if has_ctx:
            S0 = (s0_ref[0], s0_ref[1])
        else:
            S0 = (jnp.zeros((HP, HP), F32), jnp.zeros((HP, HP), F32))
        finals = lax.fori_loop(0, n, scan, S0)
        if not has_ctx:
            sfin_ref[0] = finals[0][0:DK, 0:DVG]
            sfin_ref[1] = finals[1][0:DK, 0:DVG]

        o = o_s[0] + o_s[1]
        z = z_ref[...]
        y = _rms(o, DVG) * onw_ref[...] * (z * _sigmoid(z))
        val = jnp.dot(y.astype(BF16), wo_ref[...], preferred_element_type=F32)

        @pl.when(h == 0)
        def _():
            oc_ref[...] = val

        @pl.when(h != 0)
        def _():
            oc_ref[...] = oc_ref[...] + val

    return kern


def _gdn2(P, l, T, nseq, rb0, s0, w, bcol):
    has_ctx = s0 is not None
    n = T // CHUNK
    hb = lambda off: pl.BlockSpec((T, HP), lambda b, h: (rb0 + b, off // HP + h))
    in_specs = [hb(OFF_GQ), hb(OFF_GK), hb(OFF_GV), hb(OFF_GZ),
                pl.BlockSpec((None, 9, HP), lambda b, h: (h, 0, 0)),
                pl.BlockSpec((None, T, 4), lambda b, h: (h, rb0 + b, 0)),
                pl.BlockSpec((None, 1, 4), lambda b, h: (h, 0, 0)),
                pl.BlockSpec((1, HP), lambda b, h: (0, 0)),
                pl.BlockSpec((None, HP, D), lambda b, h: (h, 0, 0))]
    args = [P, P, P, P, w['cw'], bcol, w['par'], w['onw'], w['wo']]
    out_shape = [jax.ShapeDtypeStruct((nseq * T, D), F32)]
    out_specs = [pl.BlockSpec((T, D), lambda b, h: (b, 0))]
    if has_ctx:
        in_specs.append(pl.BlockSpec((None, None, 2, None, HP, HP), lambda b, h: (b, l, 0, h, 0, 0)))
        args.append(s0)
    else:
        out_shape.append(jax.ShapeDtypeStruct((nseq, 2, H_GDN, DK, DVG), F32))
        out_specs.append(pl.BlockSpec((None, 2, None, DK, DVG), lambda b, h: (b, 0, h, 0, 0)))
    seq = lambda: pltpu.VMEM((T, HP), F32)
    both = lambda: pltpu.VMEM((2, T, HP), F32)
    return pl.pallas_call(
        _make_gdn_kernel2(T, has_ctx),
        grid=(nseq, H_GDN),
        in_specs=in_specs, out_specs=out_specs, out_shape=out_shape,
        scratch_shapes=[seq(), seq(), seq(), both(), both(), both(), both(), both(), both(), both(),
                        pltpu.VMEM((2, n, 8, HP), F32), both()],
        compiler_params=_cparams(("arbitrary", "arbitrary")),
        name="gdn_ctx" if has_ctx else "gdn_prompt",
    )(*args)


def _gdn(P, l, T, nseq, rb0, s0, w, bcol, brow):
    has_ctx = s0 is not None
    n = T // CHUNK
    hb = lambda off: pl.BlockSpec((T, HP), lambda b, h: (rb0 + b, off // HP + h))
    in_specs = [hb(OFF_GQ), hb(OFF_GK), hb(OFF_GV), hb(OFF_GZ),
                pl.BlockSpec((None, 9, HP), lambda b, h: (h, 0, 0)),
                pl.BlockSpec((None, T, 4), lambda b, h: (h, rb0 + b, 0)),
                pl.BlockSpec((None, n, 4, CHUNK), lambda b, h: (h, rb0 + b, 0, 0)),
                pl.BlockSpec((None, 1, 4), lambda b, h: (h, 0, 0)),
                pl.BlockSpec((1, HP), lambda b, h: (0, 0)),
                pl.BlockSpec((None, HP, D), lambda b, h: (h, 0, 0))]
    args = [P, P, P, P, w['cw'], bcol, brow, w['par'], w['onw'], w['wo']]
    out_shape = [jax.ShapeDtypeStruct((nseq * T, D), F32)]
    out_specs = [pl.BlockSpec((T, D), lambda b, h: (b, 0))]
    if has_ctx:
        in_specs.append(pl.BlockSpec((None, None, 2, None, HP, HP), lambda b, h: (b, l, 0, h, 0, 0)))
        args.append(s0)
    else:
        out_shape.append(jax.ShapeDtypeStruct((nseq, 2, H_GDN, DK, DVG), F32))
        out_specs.append(pl.BlockSpec((None, 2, None, DK, DVG), lambda b, h: (b, 0, h, 0, 0)))
    return pl.pallas_call(
        _make_gdn_kernel(T, has_ctx),
        grid=(nseq, H_GDN),
        in_specs=in_specs, out_specs=out_specs, out_shape=out_shape,
        scratch_shapes=[pltpu.VMEM((T, HP), F32), pltpu.VMEM((T, HP), F32), pltpu.VMEM((T, HP), F32),
                        pltpu.VMEM((2, T, HP), F32), pltpu.VMEM((2, T, HP), F32), pltpu.VMEM((T, HP), F32)],
        compiler_params=_cparams(("arbitrary", "arbitrary")),
        name="gdn_ctx" if has_ctx else "gdn_prompt",
    )(*args)


def _mix_kernel(x_ref, oap_ref, obp_ref, ocp_ref, oas_ref, obs_ref, ocs_ref, ga_ref, gb_ref, gc_ref, wout_ref,
                gt1_ref, n2w_ref, sh2_ref, sc2_ref, rw_ref, rb_ref,
                x1_ref, h2_ref, ei_ref, rk_ref, gw_ref, cnt_ref, carry_s):
    i = pl.program_id(0)

    @pl.when(i == 0)
    def _():
        carry_s[...] = jnp.zeros_like(carry_s)

    is_p = i < N_RB_PROMPT
    oa = jnp.where(is_p, oap_ref[...], oas_ref[...])
    ob = jnp.where(is_p, obp_ref[...], obs_ref[...])
    oc = jnp.where(is_p, ocp_ref[...], ocs_ref[...])
    m = _sigmoid(ga_ref[...]) * oa + _sigmoid(gb_ref[...]) * ob + _sigmoid(gc_ref[...]) * oc
    x1 = x_ref[...] + gt1_ref[...] * jnp.dot(m.astype(BF16), wout_ref[...], preferred_element_type=F32)
    x1_ref[...] = x1
    h2 = _rms(x1) * n2w_ref[...]
    h2 = h2 * (1.0 + sc2_ref[...]) + sh2_ref[...]
    h2_ref[...] = h2
    logits = _dot_hi(h2, rw_ref[...]) + rb_ref[...]

    lane = lax.broadcasted_iota(jnp.int32, (ROWS, N_EXP), 1).astype(F32)
    vals, idxs, sels = [], [], []
    cur = logits
    for _ in range(TOP_K):
        mx = jnp.max(cur, axis=-1, keepdims=True)
        idx = jnp.min(jnp.where(cur == mx, lane, float(N_EXP)), axis=-1, keepdims=True)
        sel = lane == idx
        vals.append(mx)
        idxs.append(idx)
        sels.append(sel)
        cur = jnp.where(sel, -jnp.inf, cur)
    es = [jnp.exp(v - vals[0]) for v in vals]
    den = es[0] + es[1] + es[2] + es[3]
    onehot = jnp.zeros((ROWS, N_EXP), F32)
    for sel in sels:
        onehot = onehot + sel.astype(F32)
    ri = lax.broadcasted_iota(jnp.int32, (ROWS, ROWS), 0)
    ci = lax.broadcasted_iota(jnp.int32, (ROWS, ROWS), 1)
    tri = (ri > ci).astype(BF16)
    rank = jnp.dot(tri, onehot.astype(BF16), preferred_element_type=F32) + carry_s[...]
    l4 = lax.broadcasted_iota(jnp.int32, (1, TOP_K), 1)
    ei = jnp.zeros((ROWS, TOP_K), F32)
    rk = jnp.zeros((ROWS, TOP_K), F32)
    gw = jnp.zeros((ROWS, TOP_K), F32)
    for k in range(TOP_K):
        pick = (l4 == k).astype(F32)
        ei = ei + idxs[k] * pick
        rk = rk + jnp.sum(jnp.where(sels[k], rank, 0.0), axis=-1, keepdims=True) * pick
        gw = gw + (es[k] / den) * pick
    ei_ref[...] = ei.astype(jnp.int32)
    rk_ref[...] = rk.astype(jnp.int32)
    gw_ref[...] = gw
    carry_s[...] = carry_s[...] + jnp.sum(onehot, axis=0, keepdims=True)
    cnt_ref[...] = carry_s[...]


def _mix(x, o_prompt, o_sample, P, mod, l, w):
    row = pl.BlockSpec((ROWS, D), lambda i: (i, 0))
    prow = pl.BlockSpec((ROWS, D), lambda i: (jnp.minimum(i, N_RB_PROMPT - 1), 0))
    srow = pl.BlockSpec((ROWS, D), lambda i: (jnp.maximum(i - N_RB_PROMPT, 0), 0))
    full = lambda shape: pl.BlockSpec(shape, lambda i: (0,) * len(shape))
    gate = lambda j: pl.BlockSpec((ROWS, D), lambda i: (i, j))
    tk = pl.BlockSpec((ROWS, TOP_K), lambda i: (i, 0))
    return pl.pallas_call(
        _mix_kernel,
        grid=(N_RB,),
        in_specs=[row, prow, prow, prow, srow, srow, srow, gate(0), gate(1), gate(2), full((D, D)),
                  _mod_spec(2), full((1, D)), _mod_spec(3), _mod_spec(4),
                  full((D, N_EXP)), full((1, N_EXP))],
        out_specs=[row, row, tk, tk, tk, full((1, N_EXP))],
        out_shape=[jax.ShapeDtypeStruct((N_TOK, D), F32), jax.ShapeDtypeStruct((N_TOK, D), F32),
                   jax.ShapeDtypeStruct((N_TOK, TOP_K), jnp.int32),
                   jax.ShapeDtypeStruct((N_TOK, TOP_K), jnp.int32),
                   jax.ShapeDtypeStruct((N_TOK, TOP_K), F32),
                   jax.ShapeDtypeStruct((1, N_EXP), F32)],
        scratch_shapes=[pltpu.VMEM((1, N_EXP), F32)],
        compiler_params=_cparams(("arbitrary",)),
        name="mix_router",
    )(x, *o_prompt, *o_sample, P, P, P, w['wout'], mod, w['n2w'], mod, mod, w['rw'], w['rb'])


ISSUE_UNROLL = 8


def _scatter_kernel(dest_ref, h_ref, xs_in_ref, xs_ref, sem):
    del xs_in_ref

    def body(t, carry):
        for k in range(TOP_K):
            d = dest_ref[TOP_K * t + k]
            pltpu.make_async_copy(h_ref.at[pl.ds(t, 1)], xs_ref.at[pl.ds(d, 1)], sem).start(priority=k % 2)
        return carry

    lax.fori_loop(0, ROWS, body, 0, unroll=ISSUE_UNROLL)
    for k in range(TOP_K):
        pltpu.make_async_copy(h_ref, xs_ref.at[pl.ds(0, ROWS)], sem).wait()


def _scatter(dest_flat, h2, xs_init):
    return pl.pallas_call(
        _scatter_kernel,
        grid=(N_RB,),
        in_specs=[pl.BlockSpec((ROWS * TOP_K,), lambda i: (i,), memory_space=pltpu.SMEM),
                  pl.BlockSpec((ROWS, D), lambda i: (i, 0)),
                  pl.BlockSpec(memory_space=pl.ANY)],
        out_specs=pl.BlockSpec(memory_space=pl.ANY),
        out_shape=jax.ShapeDtypeStruct((P_ROWS, D), F32),
        scratch_shapes=[pltpu.SemaphoreType.DMA(())],
        input_output_aliases={2: 0},
        compiler_params=_cparams(("arbitrary",)),
        name="moe_scatter",
    )(dest_flat, h2, xs_init)


def _make_moe_kernel(l):
    return functools.partial(_moe_kernel, l)


def _moe_kernel(l, be_ref, nu_ref, first_ref, slot_ref, nxt_ref, xs_ref, wgu_hbm, bgu_ref, wdn_hbm, bdn_ref,
                yb_ref, wgu_f, wdn_f, wgu_s, wdn_s, sems):
    i = pl.program_id(0)

    def weight_copies(e, slot):
        return (pltpu.make_async_copy(wgu_hbm.at[l, e], wgu_f.at[slot], sems.at[0, slot]),
                pltpu.make_async_copy(wdn_hbm.at[l, e], wdn_f.at[slot], sems.at[1, slot]))

    @pl.when(i == 0)
    def _():
        for cp in weight_copies(be_ref[0], 0):
            cp.start()

    @pl.when(first_ref[i] == 1)
    def _():
        slot = slot_ref[i]

        @pl.when(nxt_ref[i] >= 0)
        def _():
            for cp in weight_copies(nxt_ref[i], 1 - slot):
                cp.start()

        for cp in weight_copies(be_ref[i], slot):
            cp.wait()
        wgu_s[...] = wgu_f[slot].astype(BF16)
        wdn_s[...] = wdn_f[slot].astype(BF16)

    @pl.when(i < nu_ref[0])
    def _():
        gu = jnp.dot(xs_ref[...].astype(BF16), wgu_s[...], preferred_element_type=F32) + bgu_ref[...]
        glu = jnp.minimum(gu[:, 0:FF], LIMIT)
        lin = jnp.clip(gu[:, FF:2 * FF], -LIMIT, LIMIT)
        act = glu * _sigmoid(ALPHA * glu) * (lin + 1.0)
        yb_ref[...] = jnp.dot(act.astype(BF16), wdn_s[...], preferred_element_type=F32) + bdn_ref[...]

    @pl.when(i >= nu_ref[0])
    def _():
        yb_ref[...] = jnp.zeros_like(yb_ref)


def _moe(sched, xs, l, w_gu, b_gu, w_dn, b_dn):
    grid_spec = pltpu.PrefetchScalarGridSpec(
        num_scalar_prefetch=5,
        grid=(NB,),
        in_specs=[pl.BlockSpec((BM, D), lambda i, be, nu, *_: (jnp.minimum(i, nu[0] - 1), 0)),
                  pl.BlockSpec(memory_space=pl.ANY),
                  pl.BlockSpec((None, None, 1, 2 * FF), lambda i, be, *_: (l, be[i], 0, 0)),
                  pl.BlockSpec(memory_space=pl.ANY),
                  pl.BlockSpec((None, None, 1, D), lambda i, be, *_: (l, be[i], 0, 0))],
        out_specs=pl.BlockSpec((BM, D), lambda i, *_: (i, 0)),
        scratch_shapes=[pltpu.VMEM((2, D, 2 * FF), F32), pltpu.VMEM((2, FF, D), F32),
                        pltpu.VMEM((D, 2 * FF), BF16), pltpu.VMEM((FF, D), BF16),
                        pltpu.SemaphoreType.DMA((2, 2))])
    return pl.pallas_call(
        _make_moe_kernel(l),
        grid_spec=grid_spec,
        out_shape=jax.ShapeDtypeStruct((P_ROWS, D), F32),
        compiler_params=_cparams(("arbitrary",)),
        name="moe_experts",
    )(*sched, xs, w_gu, b_gu.reshape(DEPTH, N_EXP, 1, 2 * FF), w_dn, b_dn.reshape(DEPTH, N_EXP, 1, D))


def _make_combine_kernel(final):
    def kern(*refs):
        if final:
            dest_ref, x1_ref, gw_ref, gt2_ref, fw_ref, yb_ref, o_ref, buf, sem = refs
        else:
            dest_ref, x1_ref, gw_ref, gt2_ref, yb_ref, o_ref, buf, sem = refs

        def body(t, carry):
            for k in range(TOP_K):
                d = dest_ref[TOP_K * t + k]
                pltpu.make_async_copy(yb_ref.at[pl.ds(d, 1)], buf.at[k, pl.ds(t, 1)], sem).start(priority=k % 2)
            return carry

        lax.fori_loop(0, ROWS, body, 0, unroll=ISSUE_UNROLL)
        for k in range(TOP_K):
            pltpu.make_async_copy(yb_ref.at[pl.ds(0, ROWS)], buf.at[k], sem).wait()
        gw = gw_ref[...]
        y = gw[:, 0:1] * buf[0]
        for k in range(1, TOP_K):
            y = y + gw[:, k:k + 1] * buf[k]
        x2 = x1_ref[...] + gt2_ref[...] * y
        if final:
            x2 = _rms(x2) * fw_ref[...]
        o_ref[...] = x2

    return kern


def _combine(dest_flat, x1, gw, mod, yb, final_w):
    final = final_w is not None
    in_specs = [pl.BlockSpec((ROWS * TOP_K,), lambda i: (i,), memory_space=pltpu.SMEM),
                pl.BlockSpec((ROWS, D), lambda i: (i, 0)),
                pl.BlockSpec((ROWS, TOP_K), lambda i: (i, 0)),
                _mod_spec(5)]
    args = [dest_flat, x1, gw, mod]
    if final:
        in_specs.append(pl.BlockSpec((1, D), lambda i: (0, 0)))
        args.append(final_w)
    in_specs.append(pl.BlockSpec(memory_space=pl.ANY))
    args.append(yb)
    return pl.pallas_call(
        _make_combine_kernel(final),
        grid=(N_RB,),
        in_specs=in_specs,
        out_specs=pl.BlockSpec((ROWS, D), lambda i: (i, 0)),
        out_shape=jax.ShapeDtypeStruct((N_TOK, D), F32),
        scratch_shapes=[pltpu.VMEM((TOP_K, ROWS, D), F32), pltpu.SemaphoreType.DMA(())],
        compiler_params=_cparams(("arbitrary",)),
        name="moe_combine_final" if final else "moe_combine",
    )(*args)


def _rope_tables():
    rows = DEC_SEQ // GRID_W
    row = np.repeat(np.arange(rows, dtype=np.float32), GRID_W)
    col = np.tile(np.arange(GRID_W, dtype=np.float32), rows)
    inv = (ROPE_THETA ** (-np.arange(ROPE_PAIRS, dtype=np.float32) / ROPE_PAIRS)).astype(np.float32)
    ar = (row[:, None] * inv).astype(np.float32)
    ac = (col[:, None] * inv).astype(np.float32)
    cr, sr, cc, sc = np.cos(ar), np.sin(ar), np.cos(ac), np.sin(ac)
    cos32 = np.concatenate([cr, cr, cc, cc], axis=1).astype(np.float32)
    sin32 = np.concatenate([-sr, sr, -sc, sc], axis=1).astype(np.float32)
    return (jnp.asarray(np.tile(cos32, (1, H_MLA))), jnp.asarray(np.tile(sin32, (1, H_MLA))),
            jnp.asarray(cos32), jnp.asarray(sin32))


def _dft_tables(T):
    c = np.arange(FNO_C)
    ang_c = 2.0 * np.pi * ((c[:, None] * c[None, :]) % FNO_C) / FNO_C
    eye = np.eye(FNO_G)
    cc = np.kron(eye, np.cos(ang_c)).astype(np.float32)
    sc = np.kron(eye, np.sin(ang_c)).astype(np.float32)
    t = np.arange(T)
    ang_t = 2.0 * np.pi * ((t[:, None] * t[None, :]) % T) / T
    return (jnp.asarray(cc), jnp.asarray(sc), jnp.asarray(np.cos(ang_t).astype(np.float32)),
            jnp.asarray(np.sin(ang_t).astype(np.float32)))


_SWAP32 = np.concatenate([np.arange(8, 16), np.arange(0, 8), np.arange(24, 32), np.arange(16, 24)])


def _pad_heads(a, n_heads, width):
    lead = a.shape[:-1]
    a = a.reshape(lead + (n_heads, width))
    a = jnp.pad(a, [(0, 0)] * len(lead) + [(0, 0), (0, HP - width)])
    return a.reshape(lead + (n_heads * HP,))


def _pack_w_in(w):
    o = np.cumsum([0, Q_RANK, KV_RANK, ROPE, FNO_W, 256, 256, 256, 256, 4, 4, 4, 4, 3 * D])
    seg = lambda j: w[:, o[j]:o[j + 1]]
    kr = seg(2)
    misc = jnp.concatenate([seg(1), kr, kr[:, _SWAP32], seg(8), seg(9), seg(10), seg(11),
                            jnp.zeros((D, 256 - KV_RANK - 2 * ROPE - 16), w.dtype)], axis=1)
    packed = jnp.concatenate([seg(12), seg(0), misc, seg(3),
                              _pad_heads(seg(4), H_GDN, DK), _pad_heads(seg(5), H_GDN, DK),
                              _pad_heads(seg(6), H_GDN, DVG), _pad_heads(seg(7), H_GDN, DVG)], axis=1)
    return packed.astype(BF16)


def _pack_w_uq(w):
    w3 = w.reshape(Q_RANK, H_MLA, NOPE + ROPE)
    nope = w3[:, :, :NOPE].reshape(Q_RANK, H_MLA * NOPE)
    pe = w3[:, :, NOPE:]
    return jnp.concatenate([nope, pe.reshape(Q_RANK, H_MLA * ROPE),
                            pe[:, :, _SWAP32].reshape(Q_RANK, H_MLA * ROPE)], axis=1).astype(BF16)


def _pack_w_ukv(w):
    w3 = w.reshape(KV_RANK, H_MLA, NOPE + DV_MLA)
    return jnp.concatenate([w3[:, :, :NOPE].reshape(KV_RANK, H_MLA * NOPE),
                            w3[:, :, NOPE:].reshape(KV_RANK, H_MLA * DV_MLA)], axis=1).astype(BF16)


def kernel(x_prompt, x_sample, cache_ckv, cache_kpe, state_delta, c, c_ctx, ada_w, ada_b, norm1_w, norm2_w, w_in, mla_qn_w, mla_w_uq, mla_kvn_w, mla_w_ukv, mla_w_o, fno_w_o, gdn_conv_w, gdn_A_log, gdn_dt_bias, gdn_onorm_w, gdn_w_o, w_out, router_w, router_b, moe_w_gu, moe_b_gu, moe_w_dn, moe_b_dn, final_norm_w):
    x = jnp.concatenate([x_prompt.reshape(N_PROMPT, D), x_sample.reshape(N_SAMPLE, D)], axis=0)
    cond8 = jnp.concatenate([c_ctx[None, :], c, jnp.zeros((8 - 1 - DEC_BATCH, D), F32)], axis=0)
    mod_all = _ada(cond8, ada_w, ada_b).reshape(DEPTH, 8, 6, 1, D)
    rope = _rope_tables()
    dft_p, dft_s = _dft_tables(SEQ), _dft_tables(DEC_SEQ)
    s0_pad = jnp.pad(state_delta, [(0, 0)] * 4 + [(0, HP - DK), (0, HP - DVG)])
    xs_zero = jnp.zeros((P_ROWS, D), F32)
    rb_s = N_PROMPT // DEC_SEQ

    ckvs, kpes, states = [], [], []
    for l in range(DEPTH):
        mod = mod_all[l]
        P = _proj_in(x, norm1_w[l][None, :], mod, _pack_w_in(w_in[l]))

        wm = {'qn': mla_qn_w[l][None, :], 'wuq': _pack_w_uq(mla_w_uq[l]), 'kvn': mla_kvn_w[l][None, :],
              'wukv': _pack_w_ukv(mla_w_ukv[l]), 'wo': mla_w_o[l].astype(BF16)}
        oa_p, ckv, kpe = _mla(P, l, SEQ, BATCH, 0, None, wm, None)
        (oa_s,) = _mla(P, l, DEC_SEQ, DEC_BATCH, rb_s, (cache_ckv, cache_kpe), wm, rope)
        ckvs.append(ckv)
        kpes.append(kpe)

        fwo = fno_w_o[l].astype(BF16)
        ob_p = _fnet(P, SEQ, BATCH, 0, dft_p, fwo)
        ob_s = _fnet(P, DEC_SEQ, DEC_BATCH, rb_s, dft_s, fwo)

        cw = gdn_conv_w[l]
        cw = jnp.stack([_pad_heads(cw[:, 256 * s:256 * (s + 1)], H_GDN, DK) for s in range(3)], axis=0)
        cw = cw.reshape(3, CONV, H_GDN, HP).transpose(2, 0, 1, 3).reshape(H_GDN, 3 * CONV, HP)
        par = jnp.concatenate([gdn_A_log[l].T, gdn_dt_bias[l].T], axis=1)[:, None, :]
        onw = jnp.pad(gdn_onorm_w[l], (0, HP - DVG))[None, :]
        gwo = jnp.pad(gdn_w_o[l].reshape(H_GDN, DVG, D), [(0, 0), (0, HP - DVG), (0, 0)]).astype(BF16)
        wg = {'cw': cw, 'par': par, 'onw': onw, 'wo': gwo}
        small = P[:, OFF_MISC + 192:OFF_MISC + 208].reshape(N_TOK, 4, H_GDN)
        bcol = small.transpose(2, 0, 1)
        oc_p, st = _gdn2(P, l, SEQ, BATCH, 0, None, wg, bcol)
        (oc_s,) = _gdn2(P, l, DEC_SEQ, DEC_BATCH, rb_s, s0_pad, wg, bcol)
        states.append(st)

        wx = {'wout': w_out[l].astype(BF16), 'n2w': norm2_w[l][None, :], 'rw': router_w[l],
              'rb': router_b[l][None, :]}
        x1, h2, ei, rk, gw, cnt = _mix(x, (oa_p, ob_p, oc_p), (oa_s, ob_s, oc_s), P, mod, l, wx)

        counts = cnt[0].astype(jnp.int32)
        padded = (counts + BM - 1) // BM * BM
        pad_end = jnp.cumsum(padded)
        pad_start = pad_end - padded
        dest = (jnp.take(pad_start, ei, axis=0) + rk).reshape(NK)
        n_used = (pad_end[-1] // BM).astype(jnp.int32)
        blk = jnp.arange(NB, dtype=jnp.int32)
        block_e = jnp.sum((pad_end[None, :] <= (blk * BM)[:, None]).astype(jnp.int32), axis=1)
        block_e = jnp.minimum(block_e, N_EXP - 1)
        block_e = jnp.where(blk < n_used, block_e, block_e[n_used - 1])
        prev_e = jnp.concatenate([jnp.full((1,), -1, jnp.int32), block_e[:-1]])
        first = jnp.logical_and(blk < n_used, block_e != prev_e).astype(jnp.int32)
        slot = (jnp.cumsum(first) - 1) % 2
        eidx = jnp.arange(N_EXP, dtype=jnp.int32)
        used = jnp.where(counts > 0, eidx, N_EXP)
        later = jnp.where(eidx[None, :] > eidx[:, None], used[None, :], N_EXP)
        nxt_of = jnp.min(later, axis=1)
        nxt_of = jnp.where(nxt_of >= N_EXP, -1, nxt_of).astype(jnp.int32)
        sched = (block_e, n_used[None], first, slot.astype(jnp.int32), jnp.take(nxt_of, block_e))

        xs = _scatter(dest, h2, xs_zero)
        yb = _moe(sched, xs, l, moe_w_gu, moe_b_gu, moe_w_dn, moe_b_dn)
        x = _combine(dest, x1, gw, mod, yb, final_norm_w[None, :] if l == DEPTH - 1 else None)

    y_prompt = x[:N_PROMPT].reshape(BATCH, SEQ, D)
    y_sample = x[N_PROMPT:].reshape(DEC_BATCH, DEC_SEQ, D)
    return (y_prompt, y_sample, jnp.stack(ckvs, axis=1), jnp.stack(kpes, axis=1), jnp.stack(states, axis=1))
```

```python
import functools

import numpy as np
import jax
import jax.numpy as jnp
from jax import lax
from jax.experimental import pallas as pl
from jax.experimental.pallas import tpu as pltpu

F32 = jnp.float32
BF16 = jnp.bfloat16
HI = lax.Precision.HIGHEST

D = 1024
BATCH, SEQ = 32, 256
DEC_BATCH, DEC_SEQ = 2, 1024
DEPTH = 2
PAST = 512
GRID_W = 64
EPS = 1e-6
N_PROMPT = BATCH * SEQ
N_SAMPLE = DEC_BATCH * DEC_SEQ
N_TOK = N_PROMPT + N_SAMPLE
ROWS = 256
N_RB = N_TOK // ROWS
N_RB_PROMPT = N_PROMPT // ROWS
RB_PER_SAMPLE = DEC_SEQ // ROWS

H_MLA, Q_RANK, KV_RANK, NOPE, ROPE, DV_MLA = 8, 256, 128, 64, 32, 64
ROPE_PAIRS = ROPE // 4
ROPE_THETA = 10000.0
QB = 256
FNO_G, FNO_C = 4, 64
FNO_W = FNO_G * FNO_C
H_GDN, DK, DVG, CONV, CHUNK = 4, 64, 64, 3, 64
HP = 128
CPB = 4
N_EXP, TOP_K, FF = 32, 4, 1024
ALPHA, LIMIT = 1.702, 7.0
BM = 256
NK = N_TOK * TOP_K
NB = NK // BM + N_EXP
P_ROWS = NB * BM

OFF_GATES, OFF_QC, OFF_MISC, OFF_UF = 0, 3072, 3328, 3584
OFF_GQ, OFF_GK, OFF_GV, OFF_GZ = 3840, 4352, 4864, 5376
PC = 5888
PC_HALF = PC // 2
VMEM_LIMIT = 52 * 1024 * 1024


def _cparams(sem):
    return pltpu.CompilerParams(dimension_semantics=sem, vmem_limit_bytes=VMEM_LIMIT)


def _sigmoid(x):
    return 1.0 / (1.0 + jnp.exp(-x))


def _softplus(x):
    return jnp.maximum(x, 0.0) + jnp.log(1.0 + jnp.exp(-jnp.abs(x)))


def _dot(a, b):
    return jnp.dot(a.astype(BF16), b.astype(BF16), preferred_element_type=F32)


def _dot_hi(a, b):
    return jnp.dot(a, b, preferred_element_type=F32, precision=HI)


def _dot_nt(a, b):
    return lax.dot_general(a.astype(BF16), b.astype(BF16), (((1,), (1,)), ((), ())),
                           preferred_element_type=F32)


def _rms(x, n=None):
    n = x.shape[-1] if n is None else n
    return x * lax.rsqrt(jnp.sum(x * x, axis=-1, keepdims=True) * (1.0 / n) + EPS)


def _cond_index(i):
    return jnp.where(i < N_RB_PROMPT, 0, 1 + (i - N_RB_PROMPT) // RB_PER_SAMPLE)


def _ada_kernel(c_ref, w_ref, b_ref, o_ref):
    c = c_ref[...]
    o_ref[...] = _dot_hi(c * _sigmoid(c), w_ref[...]) + b_ref[...]


def _ada(cond8, ada_w, ada_b):
    nc = 4
    cb = 6 * D // nc
    return pl.pallas_call(
        _ada_kernel,
        grid=(DEPTH, nc),
        in_specs=[pl.BlockSpec((8, D), lambda l, j: (0, 0)),
                  pl.BlockSpec((None, D, cb), lambda l, j: (l, 0, j)),
                  pl.BlockSpec((None, 1, cb), lambda l, j: (l, 0, j))],
        out_specs=pl.BlockSpec((None, 8, cb), lambda l, j: (l, 0, j)),
        out_shape=jax.ShapeDtypeStruct((DEPTH, 8, 6 * D), F32),
        compiler_params=_cparams(("arbitrary", "arbitrary")),
        name="ada_mod",
    )(cond8, ada_w, ada_b.reshape(DEPTH, 1, 6 * D))


def _mod_spec(which, two_d=False):
    if two_d:
        return pl.BlockSpec((None, None, 1, D), lambda j, i: (_cond_index(i), which, 0, 0))
    return pl.BlockSpec((None, None, 1, D), lambda i: (_cond_index(i), which, 0, 0))


def _proj_in_kernel(x_ref, nw_ref, sh_ref, sc_ref, w_ref, o_ref):
    h = _rms(x_ref[...]) * nw_ref[...]
    h = h * (1.0 + sc_ref[...]) + sh_ref[...]
    o_ref[...] = jnp.dot(h.astype(BF16), w_ref[...], preferred_element_type=F32)


def _proj_in(x, norm_w, mod, w_packed):
    return pl.pallas_call(
        _proj_in_kernel,
        grid=(2, N_RB),
        in_specs=[pl.BlockSpec((ROWS, D), lambda j, i: (i, 0)),
                  pl.BlockSpec((1, D), lambda j, i: (0, 0)),
                  _mod_spec(0, True), _mod_spec(1, True),
                  pl.BlockSpec((D, PC_HALF), lambda j, i: (0, j))],
        out_specs=pl.BlockSpec((ROWS, PC_HALF), lambda j, i: (i, j)),
        out_shape=jax.ShapeDtypeStruct((N_TOK, PC), F32),
        compiler_params=_cparams(("arbitrary", "arbitrary")),
        name="proj_in",
    )(x, norm_w, mod, mod, w_packed)


def _make_mla_kernel(T, has_ctx):
    scale = (NOPE + ROPE) ** -0.5

    def kern(*refs):
        if has_ctx:
            (qc_ref, misc_ref, cckv_ref, ckpe_ref, qnw_ref, wuq_ref, kvnw_ref, wukv_ref, wo_ref,
             cq_ref, sq_ref, ck_ref, sk_ref, o_ref) = refs
        else:
            (qc_ref, misc_ref, qnw_ref, wuq_ref, kvnw_ref, wukv_ref, wo_ref,
             o_ref, ckv_out, kpe_out) = refs
        qn = _rms(qc_ref[...]) * qnw_ref[...]
        qa = jnp.dot(qn.astype(BF16), wuq_ref[...], preferred_element_type=F32)
        misc = misc_ref[...]
        ckv = _rms(misc[:, 0:KV_RANK]) * kvnw_ref[...]
        kr = misc[:, KV_RANK:KV_RANK + ROPE]
        q_pe = qa[:, 512:768]
        if has_ctx:
            q_pe = q_pe * cq_ref[...] + qa[:, 768:1024] * sq_ref[...]
            kr = kr * ck_ref[...] + misc[:, KV_RANK + ROPE:KV_RANK + 2 * ROPE] * sk_ref[...]
            keys_ckv = jnp.concatenate([cckv_ref[...], ckv], axis=0)
            keys_pe = jnp.concatenate([ckpe_ref[...], kr], axis=0)
        else:
            keys_ckv, keys_pe = ckv, kr
            ckv_out[...] = ckv
            kpe_out[...] = kr
        kv = jnp.dot(keys_ckv.astype(BF16), wukv_ref[...], preferred_element_type=F32)
        q_nope = qa[:, 0:512].astype(BF16)
        q_pe = q_pe.astype(BF16)
        kpe_b = keys_pe.astype(BF16)
        kv_b = kv.astype(BF16)
        for qb in range(T // QB):
            r = slice(qb * QB, (qb + 1) * QB)
            acc = jnp.zeros((QB, D), F32)
            for h in range(H_MLA):
                s = (_dot_nt(q_nope[r, NOPE * h:NOPE * (h + 1)], kv_b[:, NOPE * h:NOPE * (h + 1)])
                     + _dot_nt(q_pe[r, ROPE * h:ROPE * (h + 1)], kpe_b)) * scale
                p = jnp.exp(s - jnp.max(s, axis=-1, keepdims=True))
                den = jnp.sum(p, axis=-1, keepdims=True)
                o_h = _dot(p, kv_b[:, 512 + DV_MLA * h:512 + DV_MLA * (h + 1)]) / den
                acc = acc + jnp.dot(o_h.astype(BF16), wo_ref[DV_MLA * h:DV_MLA * (h + 1), :],
                                    preferred_element_type=F32)
            o_ref[r, :] = acc

    return kern


def _mla(P, l, T, nseq, rb0, ctx, w, rope):
    has_ctx = ctx is not None
    qc_blk, misc_blk = OFF_QC // 256, OFF_MISC // 256
    full = lambda shape: pl.BlockSpec(shape, lambda b: (0,) * len(shape))
    in_specs = [pl.BlockSpec((T, 256), lambda b: (rb0 + b, qc_blk)),
                pl.BlockSpec((T, 256), lambda b: (rb0 + b, misc_blk))]
    args = [P, P]
    if has_ctx:
        in_specs += [pl.BlockSpec((None, None, PAST, KV_RANK), lambda b: (b, l, 0, 0)),
                     pl.BlockSpec((None, None, PAST, ROPE), lambda b: (b, l, 0, 0))]
        args += [ctx[0], ctx[1]]
    in_specs += [full((1, Q_RANK)), full((Q_RANK, 1024)), full((1, KV_RANK)), full((KV_RANK, 1024)),
                 full((512, D))]
    args += [w['qn'], w['wuq'], w['kvn'], w['wukv'], w['wo']]
    out_shape = [jax.ShapeDtypeStruct((nseq * T, D), F32)]
    out_specs = [pl.BlockSpec((T, D), lambda b: (b, 0))]
    if has_ctx:
        in_specs += [full((T, 256)), full((T, 256)), full((T, ROPE)), full((T, ROPE))]
        args += list(rope)
    else:
        out_shape += [jax.ShapeDtypeStruct((nseq, T, KV_RANK), F32),
                      jax.ShapeDtypeStruct((nseq, T, ROPE), F32)]
        out_specs += [pl.BlockSpec((None, T, KV_RANK), lambda b: (b, 0, 0)),
                      pl.BlockSpec((None, T, ROPE), lambda b: (b, 0, 0))]
    return pl.pallas_call(
        _make_mla_kernel(T, has_ctx),
        grid=(nseq,),
        in_specs=in_specs, out_specs=out_specs, out_shape=out_shape,
        compiler_params=_cparams(("arbitrary",)),
        name="mla_ctx" if has_ctx else "mla_prompt",
    )(*args)


def _make_fnet_kernel(T):
    scale = 1.0 / float(np.sqrt(T * FNO_C))

    def kern(u_ref, cc_ref, sc_ref, ct_ref, st_ref, wo_ref, o_ref):
        u = u_ref[...]
        a = _dot_hi(u, cc_ref[...])
        b = _dot_hi(u, sc_ref[...])
        f = (_dot_hi(ct_ref[...], a) - _dot_hi(st_ref[...], b)) * scale
        o_ref[...] = jnp.dot(f.astype(BF16), wo_ref[...], preferred_element_type=F32)

    return kern


def _fnet(P, T, nseq, rb0, tabs, wo):
    full = lambda shape: pl.BlockSpec(shape, lambda b: (0,) * len(shape))
    return pl.pallas_call(
        _make_fnet_kernel(T),
        grid=(nseq,),
        in_specs=[pl.BlockSpec((T, FNO_W), lambda b: (rb0 + b, OFF_UF // 256)),
                  full((FNO_W, FNO_W)), full((FNO_W, FNO_W)), full((T, T)), full((T, T)),
                  full((FNO_W, D))],
        out_specs=pl.BlockSpec((T, D), lambda b: (b, 0)),
        out_shape=jax.ShapeDtypeStruct((nseq * T, D), F32),
        compiler_params=_cparams(("arbitrary",)),
        name=f"fnet_{T}",
    )(P, tabs[0], tabs[1], tabs[2], tabs[3], wo)


def _split_lhs(x):
    hi = x.astype(BF16)
    lo = (x - hi.astype(F32)).astype(BF16)
    return jnp.concatenate([hi, lo], axis=1)


def _split_rhs(p):
    hi = p.astype(BF16)
    lo = (p - hi.astype(F32)).astype(BF16)
    return jnp.concatenate([hi, lo, hi, lo], axis=0)


def _split3_rows(g):
    g1 = g.astype(BF16)
    r = g - g1.astype(F32)
    g2 = r.astype(BF16)
    g3 = (r - g2.astype(F32)).astype(BF16)
    return jnp.concatenate([g1, g2, g3, jnp.zeros_like(g1)], axis=0)


def _make_gdn_kernel(T, has_ctx):
    n = T // CHUNK
    C = CHUNK
    bf = lambda t: t.astype(BF16)

    def kern(*refs):
        if has_ctx:
            (q_ref, k_ref, v_ref, z_ref, cw_ref, bcol_ref, par_ref, onw_ref, wo_ref, s0_ref,
             oc_ref, qh_s, kh_s, vh_s, g_s, b_s, u_s, w_s, qk_s, qe_s, kd_s, egl_s, o_s) = refs
        else:
            (q_ref, k_ref, v_ref, z_ref, cw_ref, bcol_ref, par_ref, onw_ref, wo_ref,
             oc_ref, sfin_ref, qh_s, kh_s, vh_s, g_s, b_s, u_s, w_s, qk_s, qe_s, kd_s, egl_s, o_s) = refs
        h = pl.program_id(1)
        rows = lax.broadcasted_iota(jnp.int32, (T, 1), 0)
        cw = cw_ref[...]

        def conv(x_ref, seg):
            x = x_ref[...]
            xp = jnp.where(rows == 0, 0.0, pltpu.roll(x, 1, 0))
            xn = jnp.where(rows == T - 1, 0.0, pltpu.roll(x, T - 1, 0))
            y = (xp * cw[3 * seg:3 * seg + 1] + x * cw[3 * seg + 1:3 * seg + 2]
                 + xn * cw[3 * seg + 2:3 * seg + 3])
            return y * _sigmoid(y)

        yq = conv(q_ref, 0)
        qh_s[...] = yq * lax.rsqrt(jnp.sum(yq * yq, axis=-1, keepdims=True) + EPS) * (DK ** -0.5)
        yk = conv(k_ref, 1)
        kh_s[...] = yk * lax.rsqrt(jnp.sum(yk * yk, axis=-1, keepdims=True) + EPS)
        vh_s[...] = conv(v_ref, 2)

        par = par_ref[...]
        bc = bcol_ref[...]
        ones = jnp.ones((1, HP), F32)
        for d in range(2):
            g = -jnp.exp(par[:, d:d + 1]) * _softplus(bc[:, 2 + d:3 + d] + par[:, 2 + d:3 + d])
            g_s[d] = g * ones
            b_s[d] = _sigmoid(bc[:, d:d + 1]) * ones

        ii = lax.broadcasted_iota(jnp.int32, (C, HP), 0)
        jj = lax.broadcasted_iota(jnp.int32, (C, HP), 1) % C
        eye = (ii == jj).astype(F32)
        masks = ((ii >= jj, ii > jj), (ii <= jj, ii < jj))
        ri = lax.broadcasted_iota(jnp.int32, (C, 2 * HP), 0)
        li = lax.broadcasted_iota(jnp.int32, (C, 2 * HP), 1)
        lc = li % C
        lv = li < 3 * C
        cum_l = (bf(jnp.logical_and(ri >= lc, lv).astype(F32)), bf(jnp.logical_and(ri <= lc, lv).astype(F32)))
        r4 = lax.broadcasted_iota(jnp.int32, (4 * C, HP), 0)
        c4 = lax.broadcasted_iota(jnp.int32, (4 * C, HP), 1) % C
        rm = r4 % C
        rv = r4 < 3 * C
        cum_r = (bf(jnp.logical_and(rm <= c4, rv).astype(F32)), bf(jnp.logical_and(rm >= c4, rv).astype(F32)))

        def prep(it, carry):
            st = []
            for j in range(CPB):
                c = it * CPB + j
                r0 = pl.multiple_of(c * C, C)
                q = qh_s[pl.ds(r0, C), :]
                k = kh_s[pl.ds(r0, C), :]
                v = vh_s[pl.ds(r0, C), :]
                for d in range(2):
                    st.append(dict(d=d, c=c, r0=r0, q=q, k=k, v=v, g=g_s[d, pl.ds(r0, C), :],
                                   beta=b_s[d, pl.ds(r0, C), :]))
            for s in st:
                g3 = _split3_rows(s['g'])
                s['gc'] = jnp.dot(cum_l[s['d']], g3, preferred_element_type=F32)
                s['gr'] = lax.dot_general(g3, cum_r[s['d']], (((0,), (0,)), ((), ())),
                                          preferred_element_type=F32)[0:C, :]
                s['kb'] = s['k'] * s['beta']
                s['gram'] = lax.dot_general(bf(jnp.concatenate([s['kb'], s['q']], axis=0)),
                                            bf(jnp.concatenate([s['k'], s['k']], axis=0)),
                                            (((1,), (1,)), ((), ())), preferred_element_type=F32)
            for s in st:
                incl, strict = masks[s['d']]
                decay = jnp.where(incl, jnp.exp(jnp.where(incl, s['gc'] - s['gr'], 0.0)), 0.0)
                a = jnp.where(strict, s['gram'][0:C] * decay, 0.0)
                s['qk'] = s['gram'][C:2 * C] * decay
                s['x'] = eye - a
                s['p'] = jnp.dot(_split_lhs(a), _split_rhs(a), preferred_element_type=F32)
            for lvl in range(5):
                for s in st:
                    if lvl < 4:
                        r = jnp.dot(_split_lhs(jnp.concatenate([s['x'], s['p']], axis=0)), _split_rhs(s['p']),
                                    preferred_element_type=F32)
                        s['x'] = s['x'] + r[0:C]
                        s['p'] = r[C:2 * C]
                    else:
                        s['x'] = s['x'] + jnp.dot(_split_lhs(s['x']), _split_rhs(s['p']),
                                                  preferred_element_type=F32)
            for s in st:
                eg = jnp.exp(s['gc'])
                s['uw'] = jnp.dot(_split_lhs(s['x']),
                                  _split_rhs(jnp.concatenate([s['v'] * s['beta'], s['kb'] * eg], axis=1)),
                                  preferred_element_type=F32)
                glast = s['gc'][C - 1:C, :] if s['d'] == 0 else s['gc'][0:1, :]
                s['qe'] = s['q'] * eg
                s['kd'] = s['k'] * jnp.exp(glast - s['gc'])
                s['egl'] = jnp.broadcast_to(jnp.exp(glast), (8, HP))
            for s in st:
                d, r0 = s['d'], s['r0']
                u_s[d, pl.ds(r0, C), :] = s['uw'][:, 0:HP]
                w_s[d, pl.ds(r0, C), :] = s['uw'][:, HP:2 * HP]
                qk_s[d, pl.ds(r0, C), :] = s['qk']
                qe_s[d, pl.ds(r0, C), :] = s['qe']
                kd_s[d, pl.ds(r0, C), :] = s['kd']
                egl_s[d, s['c']] = s['egl']
            return carry

        lax.fori_loop(0, n // CPB, prep, 0)

        def scan(i, Ss):
            loaded = []
            for d in range(2):
                c = i if d == 0 else n - 1 - i
                r0 = pl.multiple_of(c * C, C)
                loaded.append((r0, w_s[d, pl.ds(r0, C), :], qe_s[d, pl.ds(r0, C), :], u_s[d, pl.ds(r0, C), :],
                               qk_s[d, pl.ds(r0, C), 0:C], kd_s[d, pl.ds(r0, C), :], egl_s[d, c]))
            rs = [jnp.dot(bf(jnp.concatenate([loaded[d][1], loaded[d][2]], axis=0)), bf(Ss[d]),
                          preferred_element_type=F32) for d in range(2)]
            v_new = [loaded[d][3] - rs[d][0:C] for d in range(2)]
            os_ = [rs[d][C:2 * C] + jnp.dot(bf(loaded[d][4]), bf(v_new[d]), preferred_element_type=F32)
                   for d in range(2)]
            out = [Ss[d] * loaded[d][6][0:1, :] + lax.dot_general(bf(loaded[d][5]), bf(v_new[d]),
                                                                 (((0,), (0,)), ((), ())),
                                                                 preferred_element_type=F32) for d in range(2)]
            for d in range(2):
                o_s[d, pl.ds(loaded[d][0], C), :] = os_[d]
            return tuple(out)

        if has_ctx:
            S0 = (s0_ref[0], s0_ref[1])
        else:
            S0 = (jnp.zeros((HP, HP), F32), jnp.zeros((HP, HP), F32))
        finals = lax.fori_loop(0, n, scan, S0)
        if not has_ctx:
            sfin_ref[0] = finals[0][0:DK, 0:DVG]
            sfin_ref[1] = finals[1][0:DK, 0:DVG]

        o = o_s[0] + o_s[1]
        z = z_ref[...]
        y = _rms(o, DVG) * onw_ref[...] * (z * _sigmoid(z))
        val = jnp.dot(y.astype(BF16), wo_ref[...], preferred_element_type=F32)

        @pl.when(h == 0)
        def _():
            oc_ref[...] = val

        @pl.when(h != 0)
        def _():
            oc_ref[...] = oc_ref[...] + val

    return kern


def _gdn(P, l, T, nseq, rb0, s0, w, bcol):
    has_ctx = s0 is not None
    n = T // CHUNK
    hb = lambda off: pl.BlockSpec((T, HP), lambda b, h: (rb0 + b, off // HP + h))
    in_specs = [hb(OFF_GQ), hb(OFF_GK), hb(OFF_GV), hb(OFF_GZ),
                pl.BlockSpec((None, 9, HP), lambda b, h: (h, 0, 0)),
                pl.BlockSpec((None, T, 4), lambda b, h: (h, rb0 + b, 0)),
                pl.BlockSpec((None, 1, 4), lambda b, h: (h, 0, 0)),
                pl.BlockSpec((1, HP), lambda b, h: (0, 0)),
                pl.BlockSpec((None, HP, D), lambda b, h: (h, 0, 0))]
    args = [P, P, P, P, w['cw'], bcol, w['par'], w['onw'], w['wo']]
    out_shape = [jax.ShapeDtypeStruct((nseq * T, D), F32)]
    out_specs = [pl.BlockSpec((T, D), lambda b, h: (b, 0))]
    if has_ctx:
        in_specs.append(pl.BlockSpec((None, None, 2, None, HP, HP), lambda b, h: (b, l, 0, h, 0, 0)))
        args.append(s0)
    else:
        out_shape.append(jax.ShapeDtypeStruct((nseq, 2, H_GDN, DK, DVG), F32))
        out_specs.append(pl.BlockSpec((None, 2, None, DK, DVG), lambda b, h: (b, 0, h, 0, 0)))
    seq = lambda: pltpu.VMEM((T, HP), F32)
    both = lambda: pltpu.VMEM((2, T, HP), F32)
    return pl.pallas_call(
        _make_gdn_kernel(T, has_ctx),
        grid=(nseq, H_GDN),
        in_specs=in_specs, out_specs=out_specs, out_shape=out_shape,
        scratch_shapes=[seq(), seq(), seq(), both(), both(), both(), both(), both(), both(), both(),
                        pltpu.VMEM((2, n, 8, HP), F32), both()],
        compiler_params=_cparams(("arbitrary", "arbitrary")),
        name="gdn_ctx" if has_ctx else "gdn_prompt",
    )(*args)


def _mix_kernel(x_ref, oap_ref, obp_ref, ocp_ref, oas_ref, obs_ref, ocs_ref, ga_ref, gb_ref, gc_ref, wout_ref,
                gt1_ref, n2w_ref, sh2_ref, sc2_ref, rw_ref, rb_ref,
                x1_ref, h2_ref, ei_ref, rk_ref, gw_ref, cnt_ref, carry_s):
    i = pl.program_id(0)

    @pl.when(i == 0)
    def _():
        carry_s[...] = jnp.zeros_like(carry_s)

    is_p = i < N_RB_PROMPT
    oa = jnp.where(is_p, oap_ref[...], oas_ref[...])
    ob = jnp.where(is_p, obp_ref[...], obs_ref[...])
    oc = jnp.where(is_p, ocp_ref[...], ocs_ref[...])
    m = _sigmoid(ga_ref[...]) * oa + _sigmoid(gb_ref[...]) * ob + _sigmoid(gc_ref[...]) * oc
    x1 = x_ref[...] + gt1_ref[...] * jnp.dot(m.astype(BF16), wout_ref[...], preferred_element_type=F32)
    x1_ref[...] = x1
    h2 = _rms(x1) * n2w_ref[...]
    h2 = h2 * (1.0 + sc2_ref[...]) + sh2_ref[...]
    h2_ref[...] = h2
    logits = _dot_hi(h2, rw_ref[...]) + rb_ref[...]

    lane = lax.broadcasted_iota(jnp.int32, (ROWS, N_EXP), 1).astype(F32)
    vals, idxs, sels = [], [], []
    cur = logits
    for _ in range(TOP_K):
        mx = jnp.max(cur, axis=-1, keepdims=True)
        idx = jnp.min(jnp.where(cur == mx, lane, float(N_EXP)), axis=-1, keepdims=True)
        sel = lane == idx
        vals.append(mx)
        idxs.append(idx)
        sels.append(sel)
        cur = jnp.where(sel, -jnp.inf, cur)
    es = [jnp.exp(v - vals[0]) for v in vals]
    den = es[0] + es[1] + es[2] + es[3]
    onehot = jnp.zeros((ROWS, N_EXP), F32)
    for sel in sels:
        onehot = onehot + sel.astype(F32)
    ri = lax.broadcasted_iota(jnp.int32, (ROWS, ROWS), 0)
    ci = lax.broadcasted_iota(jnp.int32, (ROWS, ROWS), 1)
    tri = (ri > ci).astype(BF16)
    rank = jnp.dot(tri, onehot.astype(BF16), preferred_element_type=F32) + carry_s[...]
    l4 = lax.broadcasted_iota(jnp.int32, (1, TOP_K), 1)
    ei = jnp.zeros((ROWS, TOP_K), F32)
    rk = jnp.zeros((ROWS, TOP_K), F32)
    gw = jnp.zeros((ROWS, TOP_K), F32)
    for k in range(TOP_K):
        pick = (l4 == k).astype(F32)
        ei = ei + idxs[k] * pick
        rk = rk + jnp.sum(jnp.where(sels[k], rank, 0.0), axis=-1, keepdims=True) * pick
        gw = gw + (es[k] / den) * pick
    ei_ref[...] = ei.astype(jnp.int32)
    rk_ref[...] = rk.astype(jnp.int32)
    gw_ref[...] = gw
    carry_s[...] = carry_s[...] + jnp.sum(onehot, axis=0, keepdims=True)
    cnt_ref[...] = carry_s[...]


def _mix(x, o_prompt, o_sample, P, mod, l, w):
    row = pl.BlockSpec((ROWS, D), lambda i: (i, 0))
    prow = pl.BlockSpec((ROWS, D), lambda i: (jnp.minimum(i, N_RB_PROMPT - 1), 0))
    srow = pl.BlockSpec((ROWS, D), lambda i: (jnp.maximum(i - N_RB_PROMPT, 0), 0))
    full = lambda shape: pl.BlockSpec(shape, lambda i: (0,) * len(shape))
    gate = lambda j: pl.BlockSpec((ROWS, D), lambda i: (i, j))
    tk = pl.BlockSpec((ROWS, TOP_K), lambda i: (i, 0))
    return pl.pallas_call(
        _mix_kernel,
        grid=(N_RB,),
        in_specs=[row, prow, prow, prow, srow, srow, srow, gate(0), gate(1), gate(2), full((D, D)),
                  _mod_spec(2), full((1, D)), _mod_spec(3), _mod_spec(4),
                  full((D, N_EXP)), full((1, N_EXP))],
        out_specs=[row, row, tk, tk, tk, full((1, N_EXP))],
        out_shape=[jax.ShapeDtypeStruct((N_TOK, D), F32), jax.ShapeDtypeStruct((N_TOK, D), F32),
                   jax.ShapeDtypeStruct((N_TOK, TOP_K), jnp.int32),
                   jax.ShapeDtypeStruct((N_TOK, TOP_K), jnp.int32),
                   jax.ShapeDtypeStruct((N_TOK, TOP_K), F32),
                   jax.ShapeDtypeStruct((1, N_EXP), F32)],
        scratch_shapes=[pltpu.VMEM((1, N_EXP), F32)],
        compiler_params=_cparams(("arbitrary",)),
        name="mix_router",
    )(x, *o_prompt, *o_sample, P, P, P, w['wout'], mod, w['n2w'], mod, mod, w['rw'], w['rb'])


ISSUE_UNROLL = 4


def _make_moe_kernel(l):
    return functools.partial(_moe_kernel, l)


def _moe_kernel(l, be_ref, nu_ref, first_ref, slot_ref, nxt_ref, nval_ref, src_cur_ref, src_nxt_ref, dst_ref,
                h_hbm, wgu_hbm, bgu_ref, wdn_hbm, bdn_ref, y_hbm,
                xbuf, ybuf, wgu_f, wdn_f, wgu_s, wdn_s, gsem, ssem, sems):
    i = pl.program_id(0)
    nu = nu_ref[0]
    cur = i % 2

    def weight_copies(e, slot):
        return (pltpu.make_async_copy(wgu_hbm.at[l, e], wgu_f.at[slot], sems.at[0, slot]),
                pltpu.make_async_copy(wdn_hbm.at[l, e], wdn_f.at[slot], sems.at[1, slot]))

    def issue_gather(idx_ref, b):
        def body(rr, carry):
            for j in range(2):
                r = 2 * rr + j
                pltpu.make_async_copy(h_hbm.at[pl.ds(idx_ref[r], 1)], xbuf.at[b, pl.ds(r, 1)],
                                      gsem.at[b]).start(priority=j)
            return carry
        lax.fori_loop(0, BM // 2, body, 0, unroll=ISSUE_UNROLL)

    def scatter_row(b, r, priority):
        pltpu.make_async_copy(ybuf.at[b, pl.ds(r, 1)], y_hbm.at[pl.ds(dst_ref[r], 1)],
                              ssem.at[b]).start(priority=priority)

    def issue_scatter(b, n):
        def body(rr, carry):
            for j in range(2):
                scatter_row(b, 2 * rr + j, j)
            return carry
        lax.fori_loop(0, n // 2, body, 0)

        @pl.when(n % 2 == 1)
        def _():
            scatter_row(b, n - 1, 0)

    def wait_gather(b):
        pltpu.make_async_copy(h_hbm.at[pl.ds(0, BM)], xbuf.at[b], gsem.at[b]).wait()

    def wait_scatter(b, n):
        size = BM
        while size >= 1:
            @pl.when((n & size) != 0)
            def _(size=size):
                pltpu.make_async_copy(ybuf.at[b, pl.ds(0, size)], y_hbm.at[pl.ds(0, size)], ssem.at[b]).wait()
            size //= 2

    @pl.when(i == 0)
    def _():
        issue_gather(src_cur_ref, 0)
        for cp in weight_copies(be_ref[0], 0):
            cp.start()

    @pl.when(i + 1 < nu)
    def _():
        issue_gather(src_nxt_ref, 1 - cur)

    @pl.when(first_ref[i] == 1)
    def _():
        slot = slot_ref[i]

        @pl.when(nxt_ref[i] >= 0)
        def _():
            for cp in weight_copies(nxt_ref[i], 1 - slot):
                cp.start()

        for cp in weight_copies(be_ref[i], slot):
            cp.wait()
        wgu_s[...] = wgu_f[slot].astype(BF16)
        wdn_s[...] = wdn_f[slot].astype(BF16)

    @pl.when(jnp.logical_and(i >= 2, i - 2 < nu))
    def _():
        wait_scatter(cur, nval_ref[jnp.maximum(i - 2, 0)])

    @pl.when(i < nu)
    def _():
        wait_gather(cur)
        gu = jnp.dot(xbuf[cur].astype(BF16), wgu_s[...], preferred_element_type=F32) + bgu_ref[...]
        glu = jnp.minimum(gu[:, 0:FF], LIMIT)
        lin = jnp.clip(gu[:, FF:2 * FF], -LIMIT, LIMIT)
        act = glu * _sigmoid(ALPHA * glu) * (lin + 1.0)
        ybuf[cur] = jnp.dot(act.astype(BF16), wdn_s[...], preferred_element_type=F32) + bdn_ref[...]
        issue_scatter(cur, nval_ref[i])

    @pl.when(i == NB - 1)
    def _():
        @pl.when(i - 1 < nu)
        def _():
            wait_scatter(1 - cur, nval_ref[jnp.maximum(i - 1, 0)])

        @pl.when(i < nu)
        def _():
            wait_scatter(cur, nval_ref[i])


def _moe(sched, src_tok, dst_row, h2, l, w_gu, b_gu, w_dn, b_dn):
    smem_blk = lambda f: pl.BlockSpec((BM,), f, memory_space=pltpu.SMEM)
    grid_spec = pltpu.PrefetchScalarGridSpec(
        num_scalar_prefetch=6,
        grid=(NB,),
        in_specs=[smem_blk(lambda i, *_: (i,)),
                  smem_blk(lambda i, *_: (jnp.minimum(i + 1, NB - 1),)),
                  smem_blk(lambda i, *_: (i,)),
                  pl.BlockSpec(memory_space=pl.ANY),
                  pl.BlockSpec(memory_space=pl.ANY),
                  pl.BlockSpec((None, None, 1, 2 * FF), lambda i, be, *_: (l, be[i], 0, 0)),
                  pl.BlockSpec(memory_space=pl.ANY),
                  pl.BlockSpec((None, None, 1, D), lambda i, be, *_: (l, be[i], 0, 0))],
        out_specs=pl.BlockSpec(memory_space=pl.ANY),
        scratch_shapes=[pltpu.VMEM((2, BM, D), F32), pltpu.VMEM((2, BM, D), F32),
                        pltpu.VMEM((2, D, 2 * FF), F32), pltpu.VMEM((2, FF, D), F32),
                        pltpu.VMEM((D, 2 * FF), BF16), pltpu.VMEM((FF, D), BF16),
                        pltpu.SemaphoreType.DMA((2,)), pltpu.SemaphoreType.DMA((2,)),
                        pltpu.SemaphoreType.DMA((2, 2))])
    return pl.pallas_call(
        _make_moe_kernel(l),
        grid_spec=grid_spec,
        out_shape=jax.ShapeDtypeStruct((NK, D), F32),
        compiler_params=_cparams(("arbitrary",)),
        name="moe_experts",
    )(*sched, src_tok, src_tok, dst_row, h2, w_gu, b_gu.reshape(DEPTH, N_EXP, 1, 2 * FF), w_dn,
      b_dn.reshape(DEPTH, N_EXP, 1, D))


def _make_combine_kernel(final):
    def kern(*refs):
        if final:
            x1_ref, gw_ref, gt2_ref, fw_ref, y0_ref, y1_ref, y2_ref, y3_ref, o_ref = refs
        else:
            x1_ref, gw_ref, gt2_ref, y0_ref, y1_ref, y2_ref, y3_ref, o_ref = refs
        gw = gw_ref[...]
        y = gw[:, 0:1] * y0_ref[...]
        for k, y_ref in ((1, y1_ref), (2, y2_ref), (3, y3_ref)):
            y = y + gw[:, k:k + 1] * y_ref[...]
        x2 = x1_ref[...] + gt2_ref[...] * y
        if final:
            x2 = _rms(x2) * fw_ref[...]
        o_ref[...] = x2

    return kern


def _combine(x1, gw, mod, y, final_w):
    final = final_w is not None
    in_specs = [pl.BlockSpec((ROWS, D), lambda i: (i, 0)),
                pl.BlockSpec((ROWS, TOP_K), lambda i: (i, 0)),
                _mod_spec(5)]
    args = [x1, gw, mod]
    if final:
        in_specs.append(pl.BlockSpec((1, D), lambda i: (0, 0)))
        args.append(final_w)
    for k in range(TOP_K):
        in_specs.append(pl.BlockSpec((ROWS, D), lambda i, k=k: (k * N_RB + i, 0)))
        args.append(y)
    return pl.pallas_call(
        _make_combine_kernel(final),
        grid=(N_RB,),
        in_specs=in_specs,
        out_specs=pl.BlockSpec((ROWS, D), lambda i: (i, 0)),
        out_shape=jax.ShapeDtypeStruct((N_TOK, D), F32),
        compiler_params=_cparams(("arbitrary",)),
        name="moe_combine_final" if final else "moe_combine",
    )(*args)


def _rope_tables():
    rows = DEC_SEQ // GRID_W
    row = np.repeat(np.arange(rows, dtype=np.float32), GRID_W)
    col = np.tile(np.arange(GRID_W, dtype=np.float32), rows)
    inv = (ROPE_THETA ** (-np.arange(ROPE_PAIRS, dtype=np.float32) / ROPE_PAIRS)).astype(np.float32)
    ar = (row[:, None] * inv).astype(np.float32)
    ac = (col[:, None] * inv).astype(np.float32)
    cr, sr, cc, sc = np.cos(ar), np.sin(ar), np.cos(ac), np.sin(ac)
    cos32 = np.concatenate([cr, cr, cc, cc], axis=1).astype(np.float32)
    sin32 = np.concatenate([-sr, sr, -sc, sc], axis=1).astype(np.float32)
    return (jnp.asarray(np.tile(cos32, (1, H_MLA))), jnp.asarray(np.tile(sin32, (1, H_MLA))),
            jnp.asarray(cos32), jnp.asarray(sin32))


def _dft_tables(T):
    c = np.arange(FNO_C)
    ang_c = 2.0 * np.pi * ((c[:, None] * c[None, :]) % FNO_C) / FNO_C
    eye = np.eye(FNO_G)
    cc = np.kron(eye, np.cos(ang_c)).astype(np.float32)
    sc = np.kron(eye, np.sin(ang_c)).astype(np.float32)
    t = np.arange(T)
    ang_t = 2.0 * np.pi * ((t[:, None] * t[None, :]) % T) / T
    return (jnp.asarray(cc), jnp.asarray(sc), jnp.asarray(np.cos(ang_t).astype(np.float32)),
            jnp.asarray(np.sin(ang_t).astype(np.float32)))


_SWAP32 = np.concatenate([np.arange(8, 16), np.arange(0, 8), np.arange(24, 32), np.arange(16, 24)])


def _pad_heads(a, n_heads, width):
    lead = a.shape[:-1]
    a = a.reshape(lead + (n_heads, width))
    a = jnp.pad(a, [(0, 0)] * len(lead) + [(0, 0), (0, HP - width)])
    return a.reshape(lead + (n_heads * HP,))


def _pack_w_in(w):
    o = np.cumsum([0, Q_RANK, KV_RANK, ROPE, FNO_W, 256, 256, 256, 256, 4, 4, 4, 4, 3 * D])
    seg = lambda j: w[:, o[j]:o[j + 1]]
    kr = seg(2)
    misc = jnp.concatenate([seg(1), kr, kr[:, _SWAP32], seg(8), seg(9), seg(10), seg(11),
                            jnp.zeros((D, 256 - KV_RANK - 2 * ROPE - 16), w.dtype)], axis=1)
    packed = jnp.concatenate([seg(12), seg(0), misc, seg(3),
                              _pad_heads(seg(4), H_GDN, DK), _pad_heads(seg(5), H_GDN, DK),
                              _pad_heads(seg(6), H_GDN, DVG), _pad_heads(seg(7), H_GDN, DVG)], axis=1)
    return packed.astype(BF16)


def _pack_w_uq(w):
    w3 = w.reshape(Q_RANK, H_MLA, NOPE + ROPE)
    nope = w3[:, :, :NOPE].reshape(Q_RANK, H_MLA * NOPE)
    pe = w3[:, :, NOPE:]
    return jnp.concatenate([nope, pe.reshape(Q_RANK, H_MLA * ROPE),
                            pe[:, :, _SWAP32].reshape(Q_RANK, H_MLA * ROPE)], axis=1).astype(BF16)


def _pack_w_ukv(w):
    w3 = w.reshape(KV_RANK, H_MLA, NOPE + DV_MLA)
    return jnp.concatenate([w3[:, :, :NOPE].reshape(KV_RANK, H_MLA * NOPE),
                            w3[:, :, NOPE:].reshape(KV_RANK, H_MLA * DV_MLA)], axis=1).astype(BF16)


def kernel(x_prompt, x_sample, cache_ckv, cache_kpe, state_delta, c, c_ctx, ada_w, ada_b, norm1_w, norm2_w, w_in, mla_qn_w, mla_w_uq, mla_kvn_w, mla_w_ukv, mla_w_o, fno_w_o, gdn_conv_w, gdn_A_log, gdn_dt_bias, gdn_onorm_w, gdn_w_o, w_out, router_w, router_b, moe_w_gu, moe_b_gu, moe_w_dn, moe_b_dn, final_norm_w):
    x = jnp.concatenate([x_prompt.reshape(N_PROMPT, D), x_sample.reshape(N_SAMPLE, D)], axis=0)
    cond8 = jnp.concatenate([c_ctx[None, :], c, jnp.zeros((8 - 1 - DEC_BATCH, D), F32)], axis=0)
    mod_all = _ada(cond8, ada_w, ada_b).reshape(DEPTH, 8, 6, 1, D)
    rope = _rope_tables()
    dft_p, dft_s = _dft_tables(SEQ), _dft_tables(DEC_SEQ)
    s0_pad = jnp.pad(state_delta, [(0, 0)] * 4 + [(0, HP - DK), (0, HP - DVG)])
    rb_s = N_PROMPT // DEC_SEQ

    ckvs, kpes, states = [], [], []
    for l in range(DEPTH):
        mod = mod_all[l]
        P = _proj_in(x, norm1_w[l][None, :], mod, _pack_w_in(w_in[l]))

        wm = {'qn': mla_qn_w[l][None, :], 'wuq': _pack_w_uq(mla_w_uq[l]), 'kvn': mla_kvn_w[l][None, :],
              'wukv': _pack_w_ukv(mla_w_ukv[l]), 'wo': mla_w_o[l].astype(BF16)}
        oa_p, ckv, kpe = _mla(P, l, SEQ, BATCH, 0, None, wm, None)
        (oa_s,) = _mla(P, l, DEC_SEQ, DEC_BATCH, rb_s, (cache_ckv, cache_kpe), wm, rope)
        ckvs.append(ckv)
        kpes.append(kpe)

        fwo = fno_w_o[l].astype(BF16)
        ob_p = _fnet(P, SEQ, BATCH, 0, dft_p, fwo)
        ob_s = _fnet(P, DEC_SEQ, DEC_BATCH, rb_s, dft_s, fwo)

        cw = gdn_conv_w[l]
        cw = jnp.stack([_pad_heads(cw[:, 256 * s:256 * (s + 1)], H_GDN, DK) for s in range(3)], axis=0)
        cw = cw.reshape(3, CONV, H_GDN, HP).transpose(2, 0, 1, 3).reshape(H_GDN, 3 * CONV, HP)
        par = jnp.concatenate([gdn_A_log[l].T, gdn_dt_bias[l].T], axis=1)[:, None, :]
        onw = jnp.pad(gdn_onorm_w[l], (0, HP - DVG))[None, :]
        gwo = jnp.pad(gdn_w_o[l].reshape(H_GDN, DVG, D), [(0, 0), (0, HP - DVG), (0, 0)]).astype(BF16)
        wg = {'cw': cw, 'par': par, 'onw': onw, 'wo': gwo}
        small = P[:, OFF_MISC + 192:OFF_MISC + 208].reshape(N_TOK, 4, H_GDN)
        bcol = small.transpose(2, 0, 1)
        oc_p, st = _gdn(P, l, SEQ, BATCH, 0, None, wg, bcol)
        (oc_s,) = _gdn(P, l, DEC_SEQ, DEC_BATCH, rb_s, s0_pad, wg, bcol)
        states.append(st)

        wx = {'wout': w_out[l].astype(BF16), 'n2w': norm2_w[l][None, :], 'rw': router_w[l],
              'rb': router_b[l][None, :]}
        x1, h2, ei, rk, gw, cnt = _mix(x, (oa_p, ob_p, oc_p), (oa_s, ob_s, oc_s), P, mod, l, wx)

        counts = cnt[0].astype(jnp.int32)
        padded = (counts + BM - 1) // BM * BM
        pad_end = jnp.cumsum(padded)
        pad_start = pad_end - padded
        dest = (jnp.take(pad_start, ei, axis=0) + rk).reshape(NK)
        n_used = (pad_end[-1] // BM).astype(jnp.int32)
        blk = jnp.arange(NB, dtype=jnp.int32)
        block_e = jnp.sum((pad_end[None, :] <= (blk * BM)[:, None]).astype(jnp.int32), axis=1)
        block_e = jnp.minimum(block_e, N_EXP - 1)
        block_e = jnp.where(blk < n_used, block_e, block_e[n_used - 1])
        prev_e = jnp.concatenate([jnp.full((1,), -1, jnp.int32), block_e[:-1]])
        first = jnp.logical_and(blk < n_used, block_e != prev_e).astype(jnp.int32)
        slot = (jnp.cumsum(first) - 1) % 2
        eidx = jnp.arange(N_EXP, dtype=jnp.int32)
        used = jnp.where(counts > 0, eidx, N_EXP)
        later = jnp.where(eidx[None, :] > eidx[:, None], used[None, :], N_EXP)
        nxt_of = jnp.min(later, axis=1)
        nxt_of = jnp.where(nxt_of >= N_EXP, -1, nxt_of).astype(jnp.int32)
        n_real = jnp.clip(jnp.take(pad_start + counts, block_e) - blk * BM, 0, BM)
        n_real = jnp.where(blk < n_used, n_real, 0).astype(jnp.int32)
        sched = (block_e, n_used[None], first, slot.astype(jnp.int32), jnp.take(nxt_of, block_e), n_real)

        slot_f = jnp.full((P_ROWS,), -1, jnp.int32).at[dest].set(jnp.arange(NK, dtype=jnp.int32),
                                                                  unique_indices=True)
        src_tok = jnp.where(slot_f >= 0, slot_f // TOP_K, 0)
        dst_row = jnp.where(slot_f >= 0, (slot_f % TOP_K) * N_TOK + slot_f // TOP_K, 0)

        y = _moe(sched, src_tok, dst_row, h2, l, moe_w_gu, moe_b_gu, moe_w_dn, moe_b_dn)
        x = _combine(x1, gw, mod, y, final_norm_w[None, :] if l == DEPTH - 1 else None)

    y_prompt = x[:N_PROMPT].reshape(BATCH, SEQ, D)
    y_sample = x[N_PROMPT:].reshape(DEC_BATCH, DEC_SEQ, D)
    return (y_prompt, y_sample, jnp.stack(ckvs, axis=1), jnp.stack(kpes, axis=1), jnp.stack(states, axis=1))
```
